```python
import math
import jax, jax.numpy as jnp
from jax import lax
import numpy as np

D_MODEL = 1024
BATCH = 2
SEQ = 16384
DEPTH = 1
DEC_BATCH = 16
DEC_SEQ = 4096
PAST_LEN = 128

A_HEADS = 8
A_DQK = 64
A_DV = 128
A_CHUNK = 128
B_HEADS = 8
B_NOPE = 128
B_ROPE = 64
B_DQK = B_NOPE + B_ROPE
B_DV = 128
Q_LORA = 384
KV_LORA = 256
ROPE_THETA = 10000.0
Q_BLOCK = 128
PEER_HEADS = 8
PEER_NKEYS = 128
PEER_EXPERTS = PEER_NKEYS * PEER_NKEYS
PEER_DQ = 256
PEER_HALF = PEER_DQ // 2
PEER_TOPK = 16
PEER_TOKEN_BLOCK = 128
EPS = 1e-6

SPLIT_SIZES = (A_HEADS * A_DQK, A_HEADS * A_DQK, A_HEADS * A_DV, A_HEADS * A_DV, 4 * A_HEADS,
               Q_LORA, KV_LORA, B_ROPE, 2 * D_MODEL)
D_IN = sum(SPLIT_SIZES)
SPLIT_POINTS = [int(v) for v in np.cumsum(SPLIT_SIZES)[:-1]]

kernel_name = "hybrid_mlstm_mla_peer_encoder"


def rmsnorm(x, g):
    xf = x.astype(jnp.float32)
    y = xf * lax.rsqrt(jnp.mean(xf * xf, axis=-1, keepdims=True) + EPS)
    return (y * g.astype(jnp.float32)).astype(x.dtype)


def _to_chunks(t):
    B, H, S = t.shape[:3]
    t = t.reshape((B, H, S // A_CHUNK, A_CHUNK) + t.shape[3:])
    return jnp.moveaxis(t, 2, 0)


def mlstm_chunkwise(q, k, v, i_pre, f_pre):
    B, H, S, dk = q.shape
    dv = v.shape[-1]
    log_f = jax.nn.log_sigmoid(f_pre.astype(jnp.float32))
    i_pre = i_pre.astype(jnp.float32)
    mask = jnp.tril(jnp.ones((A_CHUNK, A_CHUNK), dtype=bool))
    xs = (_to_chunks(q), _to_chunks(k), _to_chunks(v), _to_chunks(i_pre), _to_chunks(log_f))

    def step(carry, inp):
        C, n, m = carry
        qc, kc, vc, ic, lfc = inp
        b = jnp.cumsum(lfc, axis=-1)
        d_log = b[..., :, None] - b[..., None, :] + ic[..., None, :]
        d_log = jnp.where(mask, d_log, -jnp.inf)
        inter = b + m[..., None]
        m_t = jnp.maximum(jnp.max(d_log, axis=-1), inter)
        dw = jnp.exp(d_log - m_t[..., None])
        inter_w = jnp.exp(inter - m_t)
        s = jnp.einsum('bhtd,bhsd->bhts', qc, kc).astype(jnp.float32) * dw
        num = jnp.einsum('bhts,bhsv->bhtv', s, vc) + inter_w[..., None] * jnp.einsum('bhtd,bhdv->bhtv', qc, C)
        den = jnp.sum(s, axis=-1) + inter_w * jnp.einsum('bhtd,bhd->bht', qc, n)
        h = num / jnp.maximum(jnp.abs(den), jnp.exp(-m_t))[..., None]
        b_last = b[..., -1]
        w_log = b_last[..., None] - b + ic
        m_new = jnp.maximum(b_last + m, jnp.max(w_log, axis=-1))
        decay = jnp.exp(b_last + m - m_new)
        ws = jnp.exp(w_log - m_new[..., None])
        C_new = decay[..., None, None] * C + jnp.einsum('bhs,bhsd,bhsv->bhdv', ws, kc, vc)
        n_new = decay[..., None] * n + jnp.einsum('bhs,bhsd->bhd', ws, kc)
        return (C_new, n_new, m_new), h

    init = (jnp.zeros((B, H, dk, dv), jnp.float32), jnp.zeros((B, H, dk), jnp.float32),
            jnp.zeros((B, H), jnp.float32))
    _, hs = lax.scan(step, init, xs)
    return jnp.moveaxis(hs, 0, 2).reshape(B, H, S, dv)


def rope_tables(S):
    pos = jnp.arange(S, dtype=jnp.float32)
    inv = ROPE_THETA ** (-jnp.arange(0, B_ROPE, 2, dtype=jnp.float32) / B_ROPE)
    ang = pos[:, None] * inv[None, :]
    ang = jnp.concatenate([ang, ang], axis=-1)
    return jnp.cos(ang)[:, None, :], jnp.sin(ang)[:, None, :]


def apply_rope(t, cos, sin):
    tn, tr = t[..., :B_NOPE], t[..., B_NOPE:]
    half = B_ROPE // 2
    rot = jnp.concatenate([-tr[..., half:], tr[..., :half]], axis=-1)
    tr = (tr.astype(jnp.float32) * cos + rot.astype(jnp.float32) * sin).astype(t.dtype)
    return jnp.concatenate([tn, tr], axis=-1)


def block_attention(q, k, v):
    B, S, H, _ = q.shape
    qh = q.transpose(0, 2, 1, 3)
    kh = k.transpose(0, 2, 1, 3)
    vh = v.transpose(0, 2, 1, 3)
    n_q = S // Q_BLOCK
    qb = jnp.moveaxis(qh.reshape(B, H, n_q, Q_BLOCK, B_DQK), 2, 0)
    scale = B_DQK ** -0.5

    def one(qblk):
        s = jnp.einsum('bhqd,bhkd->bhqk', qblk, kh).astype(jnp.float32) * scale
        p = jax.nn.softmax(s, axis=-1).astype(vh.dtype)
        return jnp.einsum('bhqk,bhkd->bhqd', p, vh)

    o = lax.map(one, qb)
    o = jnp.moveaxis(o, 0, 2).reshape(B, H, S, B_DV)
    return o.transpose(0, 2, 1, 3).reshape(B, S, H * B_DV)


def peer_ffn(h, w_peer_q, peer_keys1, peer_keys2, peer_u, peer_v):
    B, S, D = h.shape
    T = B * S
    hf = h.reshape(T, D)
    q = (hf @ w_peer_q).reshape(T, PEER_HEADS, PEER_DQ)
    q1, q2 = q[..., :PEER_HALF], q[..., PEER_HALF:]
    s1 = jnp.einsum('thd,nd->thn', q1, peer_keys1).astype(jnp.float32)
    s2 = jnp.einsum('thd,nd->thn', q2, peer_keys2).astype(jnp.float32)
    v1, i1 = lax.top_k(s1, PEER_TOPK)
    v2, i2 = lax.top_k(s2, PEER_TOPK)
    cand = (v1[..., :, None] + v2[..., None, :]).reshape(T, PEER_HEADS, PEER_TOPK * PEER_TOPK)
    cand_idx = (i1[..., :, None] * PEER_NKEYS + i2[..., None, :]).reshape(T, PEER_HEADS, PEER_TOPK * PEER_TOPK)
    best, pos = lax.top_k(cand, PEER_TOPK)
    idx = jnp.take_along_axis(cand_idx, pos, axis=-1)
    w = jax.nn.softmax(best, axis=-1)
    n_blk = T // PEER_TOKEN_BLOCK
    xb = hf.reshape(n_blk, PEER_TOKEN_BLOCK, D)
    ib = idx.reshape(n_blk, PEER_TOKEN_BLOCK, PEER_HEADS, PEER_TOPK)
    wb = w.reshape(n_blk, PEER_TOKEN_BLOCK, PEER_HEADS, PEER_TOPK)

    def one(args):
        xt, it, wt = args
        u = peer_u[it]
        a = jax.nn.gelu(jnp.einsum('thkd,td->thk', u, xt).astype(jnp.float32), approximate=False)
        return jnp.einsum('thk,thkd->td', (wt * a).astype(peer_v.dtype), peer_v[it])

    out = lax.map(one, (xb, ib, wb))
    return out.reshape(B, S, D).astype(h.dtype)


def encoder_layer(x, norm_mix_g, w_in, b_mgate, mlstm_norm_g, q_a_norm_g, w_q_up, kv_a_norm_g, w_kv_up,
                  qk_norm_q_g, qk_norm_k_g, w_proj_a, w_proj_b, w_out, norm_ffn_g, w_peer_q,
                  peer_keys1, peer_keys2, peer_u, peer_v):
    B, S, D = x.shape
    h = rmsnorm(x, norm_mix_g)
    proj = jnp.einsum('bsd,de->bse', h, w_in)
    q_a, k_a, v_a, o_a, gate_a, c_q, c_kv, k_r, g_br = jnp.split(proj, SPLIT_POINTS, axis=-1)

    def heads(t, dh):
        return t.reshape(B, S, A_HEADS, dh).transpose(0, 2, 1, 3)
    qA = heads(q_a, A_DQK) * (A_DQK ** -0.5)
    kA = heads(k_a, A_DQK)
    vA = heads(v_a, A_DV)
    gates = gate_a.astype(jnp.float32).reshape(B, S, 4, A_HEADS) + b_mgate.astype(jnp.float32)
    gates = gates.transpose(2, 0, 3, 1)
    h_fwd = mlstm_chunkwise(qA, kA, vA, gates[0], gates[1])
    flip = lambda t: jnp.flip(t, axis=2)
    h_bwd = flip(mlstm_chunkwise(flip(qA), flip(kA), flip(vA), flip(gates[2]), flip(gates[3])))
    hA = rmsnorm(h_fwd + h_bwd, mlstm_norm_g[None, :, None, :])
    hA = hA.transpose(0, 2, 1, 3).reshape(B, S, A_HEADS * A_DV).astype(x.dtype) * jax.nn.sigmoid(o_a)
    yA = hA @ w_proj_a

    c_q = rmsnorm(c_q, q_a_norm_g)
    qB = (c_q @ w_q_up).reshape(B, S, B_HEADS, B_DQK)
    c_kv = rmsnorm(c_kv, kv_a_norm_g)
    kv = (c_kv @ w_kv_up).reshape(B, S, B_HEADS, B_NOPE + B_DV)
    k_nope, vB = kv[..., :B_NOPE], kv[..., B_NOPE:]
    kB = jnp.concatenate([k_nope, jnp.broadcast_to(k_r[:, :, None, :], (B, S, B_HEADS, B_ROPE))], axis=-1)
    qB = rmsnorm(qB, qk_norm_q_g)
    kB = rmsnorm(kB, qk_norm_k_g)
    cos, sin = rope_tables(S)
    qB = apply_rope(qB, cos, sin)
    kB = apply_rope(kB, cos, sin)
    yB = block_attention(qB, kB, vB) @ w_proj_b

    g = jax.nn.sigmoid(g_br)
    merged = g[..., :D_MODEL] * yA + g[..., D_MODEL:] * yB
    x = x + merged @ w_out

    x = x + peer_ffn(rmsnorm(x, norm_ffn_g), w_peer_q, peer_keys1, peer_keys2, peer_u, peer_v)
    return x


def setup_inputs(seed: int = 0) -> dict:
    key = jax.random.key(seed)
    ks = jax.random.split(key, 24)
    L = DEPTH
    f32 = jnp.float32

    def nrm(k, shape, scale):
        return jax.random.normal(k, shape, f32) * scale

    gate_base = jnp.array([-1.0, 3.0, -1.0, 3.0], f32)[None, :, None]
    return {
        "x_prompt": nrm(ks[0], (BATCH, SEQ, D_MODEL), 1.0),
        "x_sample": nrm(ks[1], (DEC_BATCH, DEC_SEQ, D_MODEL), 1.0),
        "norm_mix_g": 1.0 + nrm(ks[2], (L, D_MODEL), 0.02),
        "w_in": nrm(ks[3], (L, D_MODEL, D_IN), D_MODEL ** -0.5),
        "b_mgate": gate_base + nrm(ks[4], (L, 4, A_HEADS), 0.5),
        "mlstm_norm_g": 1.0 + nrm(ks[5], (L, A_HEADS, A_DV), 0.02),
        "q_a_norm_g": 1.0 + nrm(ks[6], (L, Q_LORA), 0.02),
        "w_q_up": nrm(ks[7], (L, Q_LORA, B_HEADS * B_DQK), Q_LORA ** -0.5),
        "kv_a_norm_g": 1.0 + nrm(ks[8], (L, KV_LORA), 0.02),
        "w_kv_up": nrm(ks[9], (L, KV_LORA, B_HEADS * (B_NOPE + B_DV)), KV_LORA ** -0.5),
        "qk_norm_q_g": 1.0 + nrm(ks[10], (L, B_DQK), 0.02),
        "qk_norm_k_g": 1.0 + nrm(ks[11], (L, B_DQK), 0.02),
        "w_proj_a": nrm(ks[12], (L, A_HEADS * A_DV, D_MODEL), (A_HEADS * A_DV) ** -0.5),
        "w_proj_b": nrm(ks[13], (L, B_HEADS * B_DV, D_MODEL), (B_HEADS * B_DV) ** -0.5),
        "w_out": nrm(ks[14], (L, D_MODEL, D_MODEL), D_MODEL ** -0.5),
        "norm_ffn_g": 1.0 + nrm(ks[15], (L, D_MODEL), 0.02),
        "w_peer_q": nrm(ks[16], (L, D_MODEL, PEER_HEADS * PEER_DQ), D_MODEL ** -0.5),
        "peer_keys1": nrm(ks[17], (L, PEER_NKEYS, PEER_HALF), PEER_HALF ** -0.5),
        "peer_keys2": nrm(ks[18], (L, PEER_NKEYS, PEER_HALF), PEER_HALF ** -0.5),
        "peer_u": nrm(ks[19], (L, PEER_EXPERTS, D_MODEL), D_MODEL ** -0.5),
        "peer_v": nrm(ks[20], (L, PEER_EXPERTS, D_MODEL), PEER_HEADS ** -0.5),
    }


def reference(x_prompt, x_sample, norm_mix_g, w_in, b_mgate, mlstm_norm_g, q_a_norm_g, w_q_up, kv_a_norm_g,
              w_kv_up, qk_norm_q_g, qk_norm_k_g, w_proj_a, w_proj_b, w_out, norm_ffn_g, w_peer_q,
              peer_keys1, peer_keys2, peer_u, peer_v):
    def run(x):
        for l in range(DEPTH):
            x = encoder_layer(x, norm_mix_g[l], w_in[l], b_mgate[l], mlstm_norm_g[l], q_a_norm_g[l], w_q_up[l],
                              kv_a_norm_g[l], w_kv_up[l], qk_norm_q_g[l], qk_norm_k_g[l], w_proj_a[l],
                              w_proj_b[l], w_out[l], norm_ffn_g[l], w_peer_q[l], peer_keys1[l], peer_keys2[l],
                              peer_u[l], peer_v[l])
        return x

    y_prompt = run(x_prompt)
    y_sample = run(x_sample)
    return (y_prompt, y_sample)
```

```python
import functools

import jax
import jax.numpy as jnp
import numpy as np
from jax import lax
from jax.experimental import pallas as pl
from jax.experimental.pallas import tpu as pltpu

F32 = jnp.float32
BF16 = jnp.bfloat16

D_MODEL = 1024
A_HEADS, A_DQK, A_DV, A_CHUNK = 8, 64, 128, 128
B_HEADS, B_NOPE, B_ROPE, B_DV = 8, 128, 64, 128
B_DQK = B_NOPE + B_ROPE
Q_LORA, KV_LORA = 384, 256
ROPE_THETA = 10000.0
PEER_HEADS, PEER_NKEYS, PEER_DQ, PEER_TOPK = 8, 128, 256, 16
PEER_HALF = PEER_DQ // 2
PEER_EXPERTS = PEER_NKEYS * PEER_NKEYS
EPS = 1e-6

LANE = 128
QK_PAD = 2 * LANE
VMEM_LIMIT = 52 * 1024 * 1024


def _cparams(*sem):
    return pltpu.CompilerParams(dimension_semantics=sem, vmem_limit_bytes=VMEM_LIMIT)


def _dot(a, b):
    return jnp.dot(a, b, preferred_element_type=F32)


def _dot_nt(a, b):
    return lax.dot_general(a, b, (((1,), (1,)), ((), ())), preferred_element_type=F32)


def _rms(x, g):
    return x * lax.rsqrt(jnp.mean(x * x, axis=-1, keepdims=True) + EPS) * g


def _log_sigmoid(x):
    return jnp.minimum(x, 0.0) - jnp.log(1.0 + jnp.exp(-jnp.abs(x)))


def _sigmoid(x):
    return 1.0 / (1.0 + jnp.exp(-x))


def _const_spec(shape):
    nd = len(shape)
    return pl.BlockSpec(shape, lambda *_: (0,) * nd)


def _inproj_kernel(x_ref, g_ref, wa_ref, wc_ref, wkg_ref, wg_ref, wkt_ref, wgt_ref, bkg_ref, bgt_ref,
                   qa_ref, va_ref, os_ref, cqkv_ref, kg_ref, gs_ref, kt_ref, gt_ref):
    h = _rms(x_ref[...], g_ref[...]).astype(BF16)
    a = _dot(h, wa_ref[...])
    qa_ref[...] = a[:, :1024].astype(BF16)
    va_ref[...] = a[:, 1024:2048].astype(BF16)
    os_ref[...] = _sigmoid(a[:, 2048:]).astype(BF16)
    cqkv_ref[...] = _dot(h, wc_ref[...])
    kg = _dot(h, wkg_ref[...]) + bkg_ref[...]
    lane = lax.broadcasted_iota(jnp.int32, kg.shape, 1)
    is_f = ((lane >= 72) & (lane < 80)) | ((lane >= 88) & (lane < 96))
    kg_ref[...] = jnp.where(is_f, _log_sigmoid(kg), kg)
    gs_ref[...] = _sigmoid(_dot(h, wg_ref[...])).astype(BF16)
    kt_ref[...] = _dot_nt(wkt_ref[...], h).astype(BF16)
    gt = _dot_nt(wgt_ref[...], h) + bgt_ref[...]
    row = lax.broadcasted_iota(jnp.int32, gt.shape, 0)
    is_fr = ((row >= 8) & (row < 16)) | (row >= 24)
    gt_ref[...] = jnp.where(is_fr, _log_sigmoid(gt), gt)


def _inproj(x2, g, wa, wc, wkg, wg, wkt, wgt, bkg, bgt, tm):
    T = x2.shape[0]
    row = lambda w: pl.BlockSpec((tm, w), lambda i: (i, 0))
    col = lambda r: pl.BlockSpec((r, tm), lambda i: (0, i))
    return pl.pallas_call(
        _inproj_kernel,
        grid=(T // tm,),
        in_specs=[row(D_MODEL), _const_spec(g.shape), _const_spec(wa.shape), _const_spec(wc.shape),
                  _const_spec(wkg.shape), _const_spec(wg.shape), _const_spec(wkt.shape),
                  _const_spec(wgt.shape), _const_spec(bkg.shape), _const_spec(bgt.shape)],
        out_specs=[row(1024), row(1024), row(1024), row(640), row(128), row(2048), col(1024), col(32)],
        out_shape=[jax.ShapeDtypeStruct((T, 1024), BF16), jax.ShapeDtypeStruct((T, 1024), BF16),
                   jax.ShapeDtypeStruct((T, 1024), BF16), jax.ShapeDtypeStruct((T, 640), F32),
                   jax.ShapeDtypeStruct((T, 128), F32), jax.ShapeDtypeStruct((T, 2048), BF16),
                   jax.ShapeDtypeStruct((1024, T), BF16), jax.ShapeDtypeStruct((32, T), F32)],
        compiler_params=_cparams("parallel"),
        name="inproj",
    )(x2, g, wa, wc, wkg, wg, wkt, wgt, bkg, bgt)


def _split_bf16(x):
    hi = x.astype(BF16)
    lo = (x - hi.astype(F32)).astype(BF16)
    return hi, lo


def _mlstm_kernel(*refs, reverse):
    if reverse:
        q_ref, kt_ref, v_ref, kg_ref, gt_ref, hf_ref, os_ref, ng_ref, out_ref, cn_scr, m_scr = refs
    else:
        q_ref, kt_ref, v_ref, kg_ref, gt_ref, out_ref, cn_scr, m_scr = refs
    L = A_CHUNK

    @pl.when(pl.program_id(1) == 0)
    def _():
        cn_scr[...] = jnp.zeros_like(cn_scr)
        m_scr[...] = jnp.zeros_like(m_scr)

    row = lax.broadcasted_iota(jnp.int32, (L, L), 0)
    col = lax.broadcasted_iota(jnp.int32, (L, L), 1)
    mask = (col >= row) if reverse else (col <= row)
    tri_c = jnp.where(mask, 1.0, 0.0).astype(BF16)
    tri_r = jnp.where((row >= col) if reverse else (row <= col), 1.0, 0.0).astype(BF16)

    g = kg_ref[:, 64:96]
    gt = gt_ref[...]
    g_hi, g_lo = _split_bf16(g)
    gt_hi, gt_lo = _split_bf16(gt)
    b_cols = _dot(tri_c, g_hi) + _dot(tri_c, g_lo)
    b_rows = _dot(gt_hi, tri_r) + _dot(gt_lo, tri_r)
    off = 16 if reverse else 0
    ones = jnp.ones((L, LANE), BF16)

    for h in range(A_HEADS):
        gi, gf = off + h, off + 8 + h
        i_r = gt[gi:gi + 1, :]
        b_r = b_rows[gf:gf + 1, :]
        b_c = b_cols[:, gf:gf + 1]
        m = m_scr[h][:, 0:1]
        d_log = jnp.where(mask, b_c - b_r + i_r, -jnp.inf)
        inter = b_c + m
        m_t = jnp.maximum(jnp.max(d_log, axis=1, keepdims=True), inter)
        dw = jnp.exp(d_log - m_t)
        inter_w = jnp.exp(inter - m_t)
        sl = slice(h * LANE, (h + 1) * LANE)
        qh = q_ref[:, sl]
        kth = kt_ref[sl, :]
        vext = jnp.concatenate([v_ref[:, sl], ones], axis=1)
        s = (_dot(qh, kth) * dw).astype(BF16)
        cn = cn_scr[h]
        tot = _dot(s, vext) + inter_w * _dot(qh, cn.astype(BF16))
        num, den = tot[:, :LANE], tot[:, LANE:]
        hh = num / jnp.maximum(jnp.abs(den), jnp.exp(-m_t))

        b_last = b_c[0:1] if reverse else b_c[L - 1:L]
        w_log = b_last - b_r + i_r
        m_new = jnp.maximum(b_last + m, jnp.max(w_log, axis=1, keepdims=True))
        decay = jnp.exp(b_last + m - m_new)
        ws = jnp.exp(w_log - m_new)
        kts = (kth.astype(F32) * ws).astype(BF16)
        cn_scr[h] = decay * cn + _dot(kts, vext)
        m_scr[h] = jnp.broadcast_to(m_new, (1, LANE))

        if reverse:
            t = hh + hf_ref[:, sl]
            y = t * lax.rsqrt(jnp.mean(t * t, axis=-1, keepdims=True) + EPS) * ng_ref[:, sl]
            out_ref[:, sl] = (y * os_ref[:, sl].astype(F32)).astype(BF16)
        else:
            out_ref[:, sl] = hh


def _mlstm(qa, kt, va, kg, gt, B, S, reverse, hf=None, osig=None, ng=None):
    L = A_CHUNK
    nc = S // L
    T = B * S
    if reverse:
        blk = lambda b, c: b * nc + (nc - 1 - c)
    else:
        blk = lambda b, c: b * nc + c
    row = lambda w: pl.BlockSpec((L, w), lambda b, c: (blk(b, c), 0))
    col = lambda r: pl.BlockSpec((r, L), lambda b, c: (0, blk(b, c)))
    in_specs = [row(1024), col(1024), row(1024), row(128), col(32)]
    args = [qa, kt, va, kg, gt]
    if reverse:
        in_specs += [row(1024), row(1024), _const_spec(ng.shape)]
        args += [hf, osig, ng]
    return pl.pallas_call(
        functools.partial(_mlstm_kernel, reverse=reverse),
        grid=(B, nc),
        in_specs=in_specs,
        out_specs=row(1024),
        out_shape=jax.ShapeDtypeStruct((T, 1024), BF16 if reverse else F32),
        scratch_shapes=[pltpu.VMEM((A_HEADS, LANE, 2 * LANE), F32), pltpu.VMEM((A_HEADS, 1, LANE), F32)],
        compiler_params=_cparams("parallel", "arbitrary"),
        name="mlstm_bwd" if reverse else "mlstm_fwd",
    )(*args)


def _rope(x, cos, sin_signed):
    lane = lax.broadcasted_iota(jnp.int32, x.shape, 1)
    half = B_ROPE // 2
    rot = jnp.where(lane < half, pltpu.roll(x, LANE - half, 1), pltpu.roll(x, half, 1))
    return x * cos + rot * sin_signed


def _mlaprep_kernel(cqkv_ref, kg_ref, cos_ref, sin_ref, gq_ref, gkv_ref, wq_ref, wk_ref, wv_ref,
                    nq_ref, nk_ref, q_ref, k_ref, v_ref):
    cq = _rms(cqkv_ref[:, :Q_LORA], gq_ref[...]).astype(BF16)
    ckv = _rms(cqkv_ref[:, Q_LORA:], gkv_ref[...]).astype(BF16)
    cos, sin = cos_ref[...], sin_ref[...]
    kg = kg_ref[...]
    lane = lax.broadcasted_iota(jnp.int32, kg.shape, 1)
    kr = jnp.where(lane < B_ROPE, kg, 0.0)
    ss_r = jnp.sum(kr * kr, axis=-1, keepdims=True)
    nq, nk = nq_ref[...], nk_ref[...]
    kr_rot = _rope(kr * nk[:, LANE:], cos, sin)
    scale = B_DQK ** -0.5
    for h in range(B_HEADS):
        q = _dot(cq, wq_ref[h])
        rq = lax.rsqrt(jnp.sum(q * q, axis=-1, keepdims=True) * (1.0 / B_DQK) + EPS) * scale
        qn = q * nq
        q_ref[0, h, :, :LANE] = (qn[:, :LANE] * rq).astype(BF16)
        q_ref[0, h, :, LANE:] = (_rope(qn[:, LANE:], cos, sin) * rq).astype(BF16)
        kn = _dot(ckv, wk_ref[h])
        rk = lax.rsqrt((jnp.sum(kn * kn, axis=-1, keepdims=True) + ss_r) * (1.0 / B_DQK) + EPS)
        k_ref[0, h, :, :LANE] = (kn * nk[:, :LANE] * rk).astype(BF16)
        k_ref[0, h, :, LANE:] = (kr_rot * rk).astype(BF16)
        v_ref[0, h] = _dot(ckv, wv_ref[h]).astype(BF16)


def _mlaprep(cqkv, kg, cos, sin, gq, gkv, wq, wk, wv, nq, nk, B, S, ts):
    ns = S // ts
    row = lambda w: pl.BlockSpec((ts, w), lambda b, i: (b * ns + i, 0))
    pos = pl.BlockSpec((ts, LANE), lambda b, i: (i, 0))
    hd = lambda w: pl.BlockSpec((1, B_HEADS, ts, w), lambda b, i: (b, 0, i, 0))
    return pl.pallas_call(
        _mlaprep_kernel,
        grid=(B, ns),
        in_specs=[row(640), row(128), pos, pos] + [_const_spec(a.shape) for a in (gq, gkv, wq, wk, wv, nq, nk)],
        out_specs=[hd(QK_PAD), hd(QK_PAD), hd(B_DV)],
        out_shape=[jax.ShapeDtypeStruct((B, B_HEADS, S, QK_PAD), BF16),
                   jax.ShapeDtypeStruct((B, B_HEADS, S, QK_PAD), BF16),
                   jax.ShapeDtypeStruct((B, B_HEADS, S, B_DV), BF16)],
        compiler_params=_cparams("parallel", "parallel"),
        name="mlaprep",
    )(cqkv, kg, cos, sin, gq, gkv, wq, wk, wv, nq, nk)


def _attn_kernel(q_ref, k_ref, v_ref, o_ref, *, tk, nk):
    q = q_ref[0, 0]
    tq = q.shape[0]

    def body(j, carry):
        m, l, acc = carry
        start = pl.multiple_of(j * tk, tk)
        k = k_ref[0, 0, pl.ds(start, tk), :]
        v = v_ref[0, 0, pl.ds(start, tk), :]
        s = _dot_nt(q, k)
        m_new = jnp.maximum(m, jnp.max(s, axis=1, keepdims=True))
        alpha = jnp.exp(m - m_new)
        p = jnp.exp(s - m_new)
        l = alpha * l + jnp.sum(p, axis=1, keepdims=True)
        acc = alpha * acc + _dot(p.astype(BF16), v)
        return m_new, l, acc

    init = (jnp.full((tq, 1), -jnp.inf, F32), jnp.zeros((tq, 1), F32), jnp.zeros((tq, B_DV), F32))
    _, l, acc = lax.fori_loop(0, nk, body, init)
    o_ref[0] = (acc / l).astype(BF16)


def _attention(q, k, v, tq, tk):
    B, H, S, _ = q.shape
    return pl.pallas_call(
        functools.partial(_attn_kernel, tk=tk, nk=S // tk),
        grid=(B, H, S // tq),
        in_specs=[pl.BlockSpec((1, 1, tq, QK_PAD), lambda b, h, i: (b, h, i, 0)),
                  pl.BlockSpec((1, 1, S, QK_PAD), lambda b, h, i: (b, h, 0, 0)),
                  pl.BlockSpec((1, 1, S, B_DV), lambda b, h, i: (b, h, 0, 0))],
        out_specs=pl.BlockSpec((1, tq, B_DV), lambda b, h, i: (b, i, h)),
        out_shape=jax.ShapeDtypeStruct((B, S, H * B_DV), BF16),
        compiler_params=_cparams("parallel", "parallel", "arbitrary"),
        name="attention",
    )(q, k, v)


def _merge_kernel(ha_ref, ao_ref, gs_ref, x_ref, wpa_ref, wpb_ref, wo_ref, o_ref):
    ya = _dot(ha_ref[...], wpa_ref[...])
    yb = _dot(ao_ref[...], wpb_ref[...])
    merged = gs_ref[:, :D_MODEL].astype(F32) * ya + gs_ref[:, D_MODEL:].astype(F32) * yb
    o_ref[...] = x_ref[...] + _dot(merged.astype(BF16), wo_ref[...])


def _merge(ha, ao, gs, x2, wpa, wpb, wo, tm):
    T = x2.shape[0]
    row = lambda w: pl.BlockSpec((tm, w), lambda i: (i, 0))
    return pl.pallas_call(
        _merge_kernel,
        grid=(T // tm,),
        in_specs=[row(1024), row(1024), row(2048), row(1024)] + [_const_spec(w.shape) for w in (wpa, wpb, wo)],
        out_specs=row(1024),
        out_shape=jax.ShapeDtypeStruct((T, D_MODEL), F32),
        compiler_params=_cparams("parallel"),
        name="merge",
    )(ha, ao, gs, x2, wpa, wpb, wo)


N_EXTRACT = PEER_TOPK + 1


def _top_rows(s, n):
    rows = []
    for _ in range(n):
        m = jnp.max(s, axis=0, keepdims=True)
        rows.append(m)
        s = jnp.where(s == m, -jnp.inf, s)
    return rows


def _stack8(rows, t):
    ri = lax.broadcasted_iota(jnp.int32, (8, t), 0)
    out = jnp.full((8, t), -jnp.inf, F32)
    for k, r in enumerate(rows):
        out = jnp.where(ri == k, r, out)
    return out


def _route_kernel(x_ref, g_ref, wq_ref, k1_ref, k2_ref, xt_ref, p1_ref, p2_ref, th_ref, q_scr):
    xn = _rms(x_ref[...], g_ref[...])
    xt_ref[...] = xn.T.astype(BF16)
    q = _dot(xn.astype(BF16), wq_ref[...])
    t = q.shape[0]
    for h in range(PEER_HEADS):
        q_scr[h] = q[:, h * PEER_DQ:(h + 1) * PEER_DQ].astype(BF16)
    k1, k2 = k1_ref[...], k2_ref[...]

    def head(h, carry):
        qh = q_scr[h]
        s1 = _dot_nt(k1, qh[:, :PEER_HALF])
        s2 = _dot_nt(k2, qh[:, PEER_HALF:])
        v1 = _top_rows(s1, N_EXTRACT)
        v2 = _top_rows(s2, N_EXTRACT)
        v2a, v2b, v2c = _stack8(v2[0:8], t), _stack8(v2[8:16], t), _stack8(v2[16:17], t)
        cand = jnp.concatenate([v1[i] + v2a for i in range(N_EXTRACT)] + [v1[0] + v2b, v1[0] + v2c], axis=0)
        best = _top_rows(cand, N_EXTRACT)
        z = jnp.zeros_like(best[0])
        for b in best[:PEER_TOPK]:
            z = z + jnp.exp(b - best[0])
        theta = 0.5 * (best[PEER_TOPK - 1] + best[PEER_TOPK])
        m1, m2 = v1[0], v2[0]
        p1_ref[h] = jnp.exp(s1 - m1) / z
        p2_ref[h] = jnp.exp(s2 - m2)
        th_ref[pl.ds(h, 1), :] = jnp.exp(theta - m1 - m2) / z
        return carry

    lax.fori_loop(0, PEER_HEADS, head, 0)


def _route(x1, g, wq, k1, k2, tr):
    T = x1.shape[0]
    return pl.pallas_call(
        _route_kernel,
        grid=(T // tr,),
        in_specs=[pl.BlockSpec((tr, D_MODEL), lambda i: (i, 0)), _const_spec(g.shape), _const_spec(wq.shape),
                  _const_spec(k1.shape), _const_spec(k2.shape)],
        out_specs=[pl.BlockSpec((D_MODEL, tr), lambda i: (0, i)),
                   pl.BlockSpec((PEER_HEADS, PEER_NKEYS, tr), lambda i: (0, 0, i)),
                   pl.BlockSpec((PEER_HEADS, PEER_NKEYS, tr), lambda i: (0, 0, i)),
                   pl.BlockSpec((PEER_HEADS, tr), lambda i: (0, i))],
        out_shape=[jax.ShapeDtypeStruct((D_MODEL, T), BF16),
                   jax.ShapeDtypeStruct((PEER_HEADS, PEER_NKEYS, T), F32),
                   jax.ShapeDtypeStruct((PEER_HEADS, PEER_NKEYS, T), F32),
                   jax.ShapeDtypeStruct((PEER_HEADS, T), F32)],
        scratch_shapes=[pltpu.VMEM((PEER_HEADS, tr, PEER_DQ), BF16)],
        compiler_params=_cparams("parallel"),
        name="route",
    )(x1, g, wq, k1, k2)


E1_PER_STEP = 8
RSQRT2 = 0.7071067811865476


def _peer_kernel(xt_ref, u_ref, vt_ref, p1_ref, p2_ref, th_ref, x1_ref, o_ref, acc_scr, g_scr):
    j = pl.program_id(1)

    @pl.when(j == 0)
    def _():
        acc_scr[...] = jnp.zeros_like(acc_scr)

    xt = xt_ref[...]
    for e in range(E1_PER_STEP):
        sl = slice(e * PEER_NKEYS, (e + 1) * PEER_NKEYS)
        a = _dot(u_ref[sl, :], xt)
        act = 0.5 * a * (1.0 + lax.erf(a * RSQRT2))
        w = jnp.zeros_like(a)
        for h in range(PEER_HEADS):
            p = p2_ref[h] * p1_ref[h, e:e + 1, :]
            w = w + jnp.where(p >= th_ref[h:h + 1, :], p, 0.0)
        g_scr[sl, :] = (w * act).astype(BF16)
    acc_scr[...] += _dot(vt_ref[...], g_scr[...])

    @pl.when(j == pl.num_programs(1) - 1)
    def _():
        o_ref[...] = x1_ref[...] + acc_scr[...].T


def _peer(xt, u, vt, p1, p2, th, x1, tb):
    T = x1.shape[0]
    eb = E1_PER_STEP * PEER_NKEYS
    return pl.pallas_call(
        _peer_kernel,
        grid=(T // tb, PEER_EXPERTS // eb),
        in_specs=[pl.BlockSpec((D_MODEL, tb), lambda i, j: (0, i)),
                  pl.BlockSpec((eb, D_MODEL), lambda i, j: (j, 0)),
                  pl.BlockSpec((D_MODEL, eb), lambda i, j: (0, j)),
                  pl.BlockSpec((PEER_HEADS, E1_PER_STEP, tb), lambda i, j: (0, j, i)),
                  pl.BlockSpec((PEER_HEADS, PEER_NKEYS, tb), lambda i, j: (0, 0, i)),
                  pl.BlockSpec((PEER_HEADS, tb), lambda i, j: (0, i)),
                  pl.BlockSpec((tb, D_MODEL), lambda i, j: (i, 0))],
        out_specs=pl.BlockSpec((tb, D_MODEL), lambda i, j: (i, 0)),
        out_shape=jax.ShapeDtypeStruct((T, D_MODEL), F32),
        scratch_shapes=[pltpu.VMEM((D_MODEL, tb), F32), pltpu.VMEM((eb, tb), BF16)],
        compiler_params=_cparams("parallel", "arbitrary"),
        name="peer",
    )(xt, u, vt, p1, p2, th, x1)


def _prep_weights(p):
    w_in = p["w_in"]
    sizes = (A_HEADS * A_DQK, A_HEADS * A_DQK, A_HEADS * A_DV, A_HEADS * A_DV, 4 * A_HEADS,
             Q_LORA, KV_LORA, B_ROPE, 2 * D_MODEL)
    pts = np.cumsum((0,) + sizes)
    w_q, w_k, w_v, w_o, w_gate, w_cq, w_ckv, w_kr, w_gbr = (w_in[:, pts[i]:pts[i + 1]] for i in range(9))

    def pad_heads(w):
        w = w.reshape(D_MODEL, A_HEADS, A_DQK)
        return jnp.pad(w, ((0, 0), (0, 0), (0, LANE - A_DQK))).reshape(D_MODEL, A_HEADS * LANE)

    out = {}
    out["wa"] = jnp.concatenate([pad_heads(w_q) * (A_DQK ** -0.5), w_v, w_o], axis=1).astype(BF16)
    out["wc"] = jnp.concatenate([w_cq, w_ckv], axis=1).astype(BF16)
    zpad = jnp.zeros((D_MODEL, LANE - B_ROPE - 4 * A_HEADS), F32)
    out["wkg"] = jnp.concatenate([w_kr, w_gate, zpad], axis=1).astype(BF16)
    out["wg"] = w_gbr.astype(BF16)
    out["wkt"] = pad_heads(w_k).T.astype(BF16)
    out["wgt"] = w_gate.T.astype(BF16)
    bias = p["b_mgate"].astype(F32).reshape(4 * A_HEADS)
    out["bkg"] = jnp.concatenate([jnp.zeros((B_ROPE,), F32), bias, jnp.zeros((32,), F32)]).reshape(1, LANE)
    out["bgt"] = bias.reshape(4 * A_HEADS, 1)
    out["g_mix"] = p["norm_mix_g"].astype(F32).reshape(1, D_MODEL)
    out["g_mlstm"] = p["mlstm_norm_g"].astype(F32).reshape(1, A_HEADS * A_DV)
    out["gq"] = p["q_a_norm_g"].astype(F32).reshape(1, Q_LORA)
    out["gkv"] = p["kv_a_norm_g"].astype(F32).reshape(1, KV_LORA)
    wq = p["w_q_up"].reshape(Q_LORA, B_HEADS, B_DQK).transpose(1, 0, 2)
    out["wq"] = jnp.pad(wq, ((0, 0), (0, 0), (0, QK_PAD - B_DQK))).astype(BF16)
    wkv = p["w_kv_up"].reshape(KV_LORA, B_HEADS, B_NOPE + B_DV).transpose(1, 0, 2)
    out["wk"] = wkv[:, :, :B_NOPE].astype(BF16)
    out["wv"] = wkv[:, :, B_NOPE:].astype(BF16)
    out["nq"] = jnp.pad(p["qk_norm_q_g"].astype(F32), (0, QK_PAD - B_DQK)).reshape(1, QK_PAD)
    out["nk"] = jnp.pad(p["qk_norm_k_g"].astype(F32), (0, QK_PAD - B_DQK)).reshape(1, QK_PAD)
    out["wpa"] = p["w_proj_a"].astype(BF16)
    out["wpb"] = p["w_proj_b"].astype(BF16)
    out["wo"] = p["w_out"].astype(BF16)
    out["g_ffn"] = p["norm_ffn_g"].astype(F32).reshape(1, D_MODEL)
    out["wpq"] = p["w_peer_q"].astype(BF16)
    out["k1"] = p["peer_keys1"].astype(BF16)
    out["k2"] = p["peer_keys2"].astype(BF16)
    out["u"] = p["peer_u"].astype(BF16)
    out["vt"] = p["peer_v"].T.astype(BF16)
    return out


def _rope_tables(S):
    pos = jnp.arange(S, dtype=F32)
    inv = ROPE_THETA ** (-jnp.arange(0, B_ROPE, 2, dtype=F32) / B_ROPE)
    ang = pos[:, None] * inv[None, :]
    ang = jnp.concatenate([ang, ang], axis=-1)
    sign = jnp.where(jnp.arange(B_ROPE) < B_ROPE // 2, -1.0, 1.0).astype(F32)
    pad = ((0, 0), (0, LANE - B_ROPE))
    return jnp.pad(jnp.cos(ang), pad), jnp.pad(jnp.sin(ang) * sign, pad)


def _pick(n, pref):
    t = min(n, pref)
    assert n % t == 0, (n, t)
    return t


def _layer(x, w):
    B, S, _ = x.shape
    T = B * S
    x2 = x.reshape(T, D_MODEL)
    qa, va, osig, cqkv, kg, gs, kt, gt = _inproj(
        x2, w["g_mix"], w["wa"], w["wc"], w["wkg"], w["wg"], w["wkt"], w["wgt"], w["bkg"], w["bgt"],
        _pick(T, 256))
    hf = _mlstm(qa, kt, va, kg, gt, B, S, False)
    ha = _mlstm(qa, kt, va, kg, gt, B, S, True, hf, osig, w["g_mlstm"])
    cos, sin = _rope_tables(S)
    q, k, v = _mlaprep(cqkv, kg, cos, sin, w["gq"], w["gkv"], w["wq"], w["wk"], w["wv"], w["nq"], w["nk"],
                       B, S, _pick(S, 256))
    ao = _attention(q, k, v, _pick(S, 512), _pick(S, 512)).reshape(T, B_HEADS * B_DV)
    x1 = _merge(ha, ao, gs, x2, w["wpa"], w["wpb"], w["wo"], _pick(T, 512))
    xt, p1, p2, th = _route(x1, w["g_ffn"], w["wpq"], w["k1"], w["k2"], _pick(T, 256))
    y = _peer(xt, w["u"], w["vt"], p1, p2, th, x1, _pick(T, 256))
    return y.reshape(B, S, D_MODEL)


def kernel(x_prompt, x_sample, norm_mix_g, w_in, b_mgate, mlstm_norm_g, q_a_norm_g, w_q_up, kv_a_norm_g, w_kv_up, qk_norm_q_g, qk_norm_k_g, w_proj_a, w_proj_b, w_out, norm_ffn_g, w_peer_q, peer_keys1, peer_keys2, peer_u, peer_v):
    params = dict(norm_mix_g=norm_mix_g, w_in=w_in, b_mgate=b_mgate, mlstm_norm_g=mlstm_norm_g,
                  q_a_norm_g=q_a_norm_g, w_q_up=w_q_up, kv_a_norm_g=kv_a_norm_g, w_kv_up=w_kv_up,
                  qk_norm_q_g=qk_norm_q_g, qk_norm_k_g=qk_norm_k_g, w_proj_a=w_proj_a, w_proj_b=w_proj_b,
                  w_out=w_out, norm_ffn_g=norm_ffn_g, w_peer_q=w_peer_q, peer_keys1=peer_keys1,
                  peer_keys2=peer_keys2, peer_u=peer_u, peer_v=peer_v)
    depth = w_in.shape[0]
    layers = [_prep_weights({k: v[l] for k, v in params.items()}) for l in range(depth)]

    def run(x):
        for w in layers:
            x = _layer(x, w)
        return x

    return (run(x_prompt), run(x_sample))
```

```python
import functools

import jax
import jax.numpy as jnp
import numpy as np
from jax import lax
from jax.experimental import pallas as pl
from jax.experimental.pallas import tpu as pltpu

F32 = jnp.float32
BF16 = jnp.bfloat16

D_MODEL = 1024
A_HEADS, A_DQK, A_DV, A_CHUNK = 8, 64, 128, 128
B_HEADS, B_NOPE, B_ROPE, B_DV = 8, 128, 64, 128
B_DQK = B_NOPE + B_ROPE
Q_LORA, KV_LORA = 384, 256
ROPE_THETA = 10000.0
PEER_HEADS, PEER_NKEYS, PEER_DQ, PEER_TOPK = 8, 128, 256, 16
PEER_HALF = PEER_DQ // 2
PEER_EXPERTS = PEER_NKEYS * PEER_NKEYS
EPS = 1e-6

LANE = 128
QK_PAD = 2 * LANE
VMEM_LIMIT = 52 * 1024 * 1024


def _cparams(*sem):
    return pltpu.CompilerParams(dimension_semantics=sem, vmem_limit_bytes=VMEM_LIMIT)


def _dot(a, b):
    return jnp.dot(a, b, preferred_element_type=F32)


def _dot_nt(a, b):
    return lax.dot_general(a, b, (((1,), (1,)), ((), ())), preferred_element_type=F32)


def _rms(x, g):
    return x * lax.rsqrt(jnp.mean(x * x, axis=-1, keepdims=True) + EPS) * g


def _log_sigmoid(x):
    return jnp.minimum(x, 0.0) - jnp.log(1.0 + jnp.exp(-jnp.abs(x)))


def _sigmoid(x):
    return 1.0 / (1.0 + jnp.exp(-x))


def _const_spec(shape):
    nd = len(shape)
    return pl.BlockSpec(shape, lambda *_: (0,) * nd)


def _inproj_kernel(x_ref, g_ref, wa_ref, wc_ref, wkg_ref, wg_ref, wkt_ref, wgt_ref, bkg_ref, bgt_ref,
                   qa_ref, va_ref, os_ref, cqkv_ref, kg_ref, gs_ref, kt_ref, gt_ref):
    h = _rms(x_ref[...], g_ref[...]).astype(BF16)
    a = _dot(h, wa_ref[...])
    qa_ref[...] = a[:, :1024].astype(BF16)
    va_ref[...] = a[:, 1024:2048].astype(BF16)
    os_ref[...] = _sigmoid(a[:, 2048:]).astype(BF16)
    cqkv_ref[...] = _dot(h, wc_ref[...])
    kg = _dot(h, wkg_ref[...]) + bkg_ref[...]
    lane = lax.broadcasted_iota(jnp.int32, kg.shape, 1)
    is_f = ((lane >= 72) & (lane < 80)) | ((lane >= 88) & (lane < 96))
    kg_ref[...] = jnp.where(is_f, _log_sigmoid(kg), kg)
    gs_ref[...] = _sigmoid(_dot(h, wg_ref[...])).astype(BF16)
    kt_ref[...] = _dot_nt(wkt_ref[...], h).astype(BF16)
    gt = _dot_nt(wgt_ref[...], h) + bgt_ref[...]
    row = lax.broadcasted_iota(jnp.int32, gt.shape, 0)
    is_fr = ((row >= 8) & (row < 16)) | (row >= 24)
    gt_ref[...] = jnp.where(is_fr, _log_sigmoid(gt), gt)


def _inproj(x2, g, wa, wc, wkg, wg, wkt, wgt, bkg, bgt, tm):
    T = x2.shape[0]
    row = lambda w: pl.BlockSpec((tm, w), lambda i: (i, 0))
    col = lambda r: pl.BlockSpec((r, tm), lambda i: (0, i))
    return pl.pallas_call(
        _inproj_kernel,
        grid=(T // tm,),
        in_specs=[row(D_MODEL), _const_spec(g.shape), _const_spec(wa.shape), _const_spec(wc.shape),
                  _const_spec(wkg.shape), _const_spec(wg.shape), _const_spec(wkt.shape),
                  _const_spec(wgt.shape), _const_spec(bkg.shape), _const_spec(bgt.shape)],
        out_specs=[row(1024), row(1024), row(1024), row(640), row(128), row(2048), col(1024), col(32)],
        out_shape=[jax.ShapeDtypeStruct((T, 1024), BF16), jax.ShapeDtypeStruct((T, 1024), BF16),
                   jax.ShapeDtypeStruct((T, 1024), BF16), jax.ShapeDtypeStruct((T, 640), F32),
                   jax.ShapeDtypeStruct((T, 128), F32), jax.ShapeDtypeStruct((T, 2048), BF16),
                   jax.ShapeDtypeStruct((1024, T), BF16), jax.ShapeDtypeStruct((32, T), F32)],
        compiler_params=_cparams("parallel"),
        name="inproj",
    )(x2, g, wa, wc, wkg, wg, wkt, wgt, bkg, bgt)


def _split_bf16(x):
    hi = x.astype(BF16)
    lo = (x - hi.astype(F32)).astype(BF16)
    return hi, lo


def _mlstm_kernel(*refs, reverse):
    if reverse:
        q_ref, kt_ref, v_ref, kg_ref, gt_ref, hf_ref, os_ref, ng_ref, out_ref, cn_scr, m_scr = refs
    else:
        q_ref, kt_ref, v_ref, kg_ref, gt_ref, out_ref, cn_scr, m_scr = refs
    L = A_CHUNK

    @pl.when(pl.program_id(1) == 0)
    def _():
        cn_scr[...] = jnp.zeros_like(cn_scr)
        m_scr[...] = jnp.zeros_like(m_scr)

    row = lax.broadcasted_iota(jnp.int32, (L, L), 0)
    col = lax.broadcasted_iota(jnp.int32, (L, L), 1)
    mask = (col >= row) if reverse else (col <= row)
    tri_c = jnp.where(mask, 1.0, 0.0).astype(BF16)
    tri_r = jnp.where((row >= col) if reverse else (row <= col), 1.0, 0.0).astype(BF16)

    g = kg_ref[:, 64:96]
    gt = gt_ref[...]
    g_hi, g_lo = _split_bf16(g)
    gt_hi, gt_lo = _split_bf16(gt)
    b_cols = _dot(tri_c, g_hi) + _dot(tri_c, g_lo)
    b_rows = _dot(gt_hi, tri_r) + _dot(gt_lo, tri_r)
    off = 16 if reverse else 0
    ones = jnp.ones((L, LANE), BF16)

    for h in range(A_HEADS):
        gi, gf = off + h, off + 8 + h
        i_r = gt[gi:gi + 1, :]
        b_r = b_rows[gf:gf + 1, :]
        b_c = b_cols[:, gf:gf + 1]
        m = m_scr[h][:, 0:1]
        d_log = jnp.where(mask, b_c - b_r + i_r, -jnp.inf)
        inter = b_c + m
        m_t = jnp.maximum(jnp.max(d_log, axis=1, keepdims=True), inter)
        dw = jnp.exp(d_log - m_t)
        inter_w = jnp.exp(inter - m_t)
        sl = slice(h * LANE, (h + 1) * LANE)
        qh = q_ref[:, sl]
        kth = kt_ref[sl, :]
        vext = jnp.concatenate([v_ref[:, sl], ones], axis=1)
        s = (_dot(qh, kth) * dw).astype(BF16)
        cn = cn_scr[h]
        tot = _dot(s, vext) + inter_w * _dot(qh, cn.astype(BF16))
        num, den = tot[:, :LANE], tot[:, LANE:]
        hh = num / jnp.maximum(jnp.abs(den), jnp.exp(-m_t))

        b_last = b_c[0:1] if reverse else b_c[L - 1:L]
        w_log = b_last - b_r + i_r
        m_new = jnp.maximum(b_last + m, jnp.max(w_log, axis=1, keepdims=True))
        decay = jnp.exp(b_last + m - m_new)
        ws = jnp.exp(w_log - m_new)
        kts = (kth.astype(F32) * ws).astype(BF16)
        cn_scr[h] = decay * cn + _dot(kts, vext)
        m_scr[h] = jnp.broadcast_to(m_new, (1, LANE))

        if reverse:
            t = hh + hf_ref[:, sl]
            y = t * lax.rsqrt(jnp.mean(t * t, axis=-1, keepdims=True) + EPS) * ng_ref[:, sl]
            out_ref[:, sl] = (y * os_ref[:, sl].astype(F32)).astype(BF16)
        else:
            out_ref[:, sl] = hh


def _mlstm(qa, kt, va, kg, gt, B, S, reverse, hf=None, osig=None, ng=None):
    L = A_CHUNK
    nc = S // L
    T = B * S
    if reverse:
        blk = lambda b, c: b * nc + (nc - 1 - c)
    else:
        blk = lambda b, c: b * nc + c
    row = lambda w: pl.BlockSpec((L, w), lambda b, c: (blk(b, c), 0))
    col = lambda r: pl.BlockSpec((r, L), lambda b, c: (0, blk(b, c)))
    in_specs = [row(1024), col(1024), row(1024), row(128), col(32)]
    args = [qa, kt, va, kg, gt]
    if reverse:
        in_specs += [row(1024), row(1024), _const_spec(ng.shape)]
        args += [hf, osig, ng]
    return pl.pallas_call(
        functools.partial(_mlstm_kernel, reverse=reverse),
        grid=(B, nc),
        in_specs=in_specs,
        out_specs=row(1024),
        out_shape=jax.ShapeDtypeStruct((T, 1024), BF16 if reverse else F32),
        scratch_shapes=[pltpu.VMEM((A_HEADS, LANE, 2 * LANE), F32), pltpu.VMEM((A_HEADS, 1, LANE), F32)],
        compiler_params=_cparams("parallel", "arbitrary"),
        name="mlstm_bwd" if reverse else "mlstm_fwd",
    )(*args)


def _rope(x, cos, sin_signed, axis):
    idx = lax.broadcasted_iota(jnp.int32, x.shape, axis)
    half = B_ROPE // 2
    rot = jnp.where(idx < half, pltpu.roll(x, LANE - half, axis), pltpu.roll(x, half, axis))
    return x * cos + rot * sin_signed


BOUND_COL = B_DQK - LANE
LOG2E = 1.4426950408889634


def _mlaprep_kernel(cqkv_ref, kg_ref, cos_ref, sin_ref, cost_ref, sint_ref, gq_ref, gkv_ref, wq_ref, wkt_ref,
                    wv_ref, nq_ref, nkt_ref, kb_ref, q_ref, kt_ref, v_ref):
    ts = cqkv_ref.shape[0]
    cq = _rms(cqkv_ref[:, :Q_LORA], gq_ref[...]).astype(BF16)
    ckv = _rms(cqkv_ref[:, Q_LORA:], gkv_ref[...]).astype(BF16)
    kgt = kg_ref[...].T
    rowi = lax.broadcasted_iota(jnp.int32, kgt.shape, 0)
    krt = jnp.where(rowi < B_ROPE, kgt, 0.0)
    ss_r = jnp.sum(krt * krt, axis=0, keepdims=True)
    nkt = nkt_ref[...]
    krt_rot = _rope(krt * nkt[LANE:], cost_ref[...], sint_ref[...], 0)
    one_row = jnp.where(rowi == BOUND_COL, 1.0, 0.0)
    nq = nq_ref[...]
    cos, sin = cos_ref[...], sin_ref[...]
    lane = lax.broadcasted_iota(jnp.int32, (ts, LANE), 1)
    scale = B_DQK ** -0.5 * LOG2E
    kb = kb_ref[...]
    ones = jnp.ones((ts, B_DV), BF16)
    for h in range(B_HEADS):
        q = _dot(cq, wq_ref[h])
        rq = lax.rsqrt(jnp.sum(q * q, axis=-1, keepdims=True) * (1.0 / B_DQK) + EPS) * scale
        qn = q * nq
        qnorm = jnp.sqrt(jnp.sum(qn * qn, axis=-1, keepdims=True)) * rq
        q_ref[0, h, :, :LANE] = (qn[:, :LANE] * rq).astype(BF16)
        q_rot = _rope(qn[:, LANE:], cos, sin, 1) * rq
        q_ref[0, h, :, LANE:] = jnp.where(lane == BOUND_COL, -(qnorm * kb), q_rot).astype(BF16)
        knt = _dot_nt(wkt_ref[h], ckv)
        rk = lax.rsqrt((jnp.sum(knt * knt, axis=0, keepdims=True) + ss_r) * (1.0 / B_DQK) + EPS)
        kt_ref[0, h, :LANE, :] = (knt * nkt[:LANE] * rk).astype(BF16)
        kt_ref[0, h, LANE:, :] = (krt_rot * rk + one_row).astype(BF16)
        v_ref[0, h, :, :B_DV] = _dot(ckv, wv_ref[h]).astype(BF16)
        v_ref[0, h, :, B_DV:] = ones


def _mlaprep(cqkv, kg, cos, sin, cost, sint, gq, gkv, wq, wkt, wv, nq, nkt, kb, B, S, ts):
    ns = S // ts
    row = lambda w: pl.BlockSpec((ts, w), lambda b, i: (b * ns + i, 0))
    pos = pl.BlockSpec((ts, LANE), lambda b, i: (i, 0))
    post = pl.BlockSpec((LANE, ts), lambda b, i: (0, i))
    consts = (gq, gkv, wq, wkt, wv, nq, nkt, kb)
    return pl.pallas_call(
        _mlaprep_kernel,
        grid=(B, ns),
        in_specs=[row(640), row(128), pos, pos, post, post] + [_const_spec(a.shape) for a in consts],
        out_specs=[pl.BlockSpec((1, B_HEADS, ts, QK_PAD), lambda b, i: (b, 0, i, 0)),
                   pl.BlockSpec((1, B_HEADS, QK_PAD, ts), lambda b, i: (b, 0, 0, i)),
                   pl.BlockSpec((1, B_HEADS, ts, 2 * B_DV), lambda b, i: (b, 0, i, 0))],
        out_shape=[jax.ShapeDtypeStruct((B, B_HEADS, S, QK_PAD), BF16),
                   jax.ShapeDtypeStruct((B, B_HEADS, QK_PAD, S), BF16),
                   jax.ShapeDtypeStruct((B, B_HEADS, S, 2 * B_DV), BF16)],
        compiler_params=_cparams("parallel", "parallel"),
        name="mlaprep",
    )(cqkv, kg, cos, sin, cost, sint, *consts)


def _attn_kernel(q_ref, kt_ref, v_ref, o_ref, *, tk, nk):
    q = q_ref[0, 0]
    tq = q.shape[0]

    def body(j, carry):
        m, acc = carry
        start = pl.multiple_of(j * tk, tk)
        kt = kt_ref[0, 0, :, pl.ds(start, tk)]
        v = v_ref[0, 0, pl.ds(start, tk), :]
        s = _dot(q, kt)
        m_new = jnp.maximum(m, jnp.max(s, axis=1, keepdims=True))
        p = jnp.exp2(s - m_new)
        acc = jnp.exp2(m - m_new) * acc + _dot(p.astype(BF16), v)
        return m_new, acc

    init = (jnp.full((tq, 1), -jnp.inf, F32), jnp.zeros((tq, 2 * B_DV), F32))
    _, acc = lax.fori_loop(0, nk, body, init)
    o_ref[0] = (acc[:, :B_DV] / acc[:, B_DV:]).astype(BF16)


def _attn_fast_kernel(q_ref, kt_ref, v_ref, o_ref, *, tk, nk, unroll):
    q = q_ref[0, 0]
    tq = q.shape[0]

    def probs(j):
        kt = kt_ref[0, 0, :, pl.ds(pl.multiple_of(j * tk, tk), tk)]
        return jnp.exp2(_dot(q, kt)).astype(BF16)

    def weighted(j, p, acc):
        v = v_ref[0, 0, pl.ds(pl.multiple_of(j * tk, tk), tk), :]
        return acc + _dot(p, v)

    def body(j, carry):
        p, acc = carry
        return probs(j), weighted(j - 1, p, acc)

    p, acc = lax.fori_loop(1, nk, body, (probs(0), jnp.zeros((tq, 2 * B_DV), F32)), unroll=unroll)
    acc = weighted(nk - 1, p, acc)
    o_ref[0] = (acc[:, :B_DV] / acc[:, B_DV:]).astype(BF16)


def _attention(q, kt, v, tq, tk, fast):
    B, H, S, _ = q.shape
    nk = S // tk
    if fast:
        body, name = functools.partial(_attn_fast_kernel, tk=tk, nk=nk, unroll=max(1, min(8, nk - 1))), "attention_fast"
    else:
        body, name = functools.partial(_attn_kernel, tk=tk, nk=nk), "attention"
    return pl.pallas_call(
        body,
        grid=(B, H, S // tq),
        in_specs=[pl.BlockSpec((1, 1, tq, QK_PAD), lambda b, h, i: (b, h, i, 0)),
                  pl.BlockSpec((1, 1, QK_PAD, S), lambda b, h, i: (b, h, 0, 0)),
                  pl.BlockSpec((1, 1, S, 2 * B_DV), lambda b, h, i: (b, h, 0, 0))],
        out_specs=pl.BlockSpec((1, tq, B_DV), lambda b, h, i: (b, i, h)),
        out_shape=jax.ShapeDtypeStruct((B, S, H * B_DV), BF16),
        compiler_params=_cparams("parallel", "parallel", "arbitrary"),
        name=name,
    )(q, kt, v)


def _merge_kernel(ha_ref, ao_ref, gs_ref, x_ref, wpa_ref, wpb_ref, wo_ref, o_ref):
    ya = _dot(ha_ref[...], wpa_ref[...])
    yb = _dot(ao_ref[...], wpb_ref[...])
    merged = gs_ref[:, :D_MODEL].astype(F32) * ya + gs_ref[:, D_MODEL:].astype(F32) * yb
    o_ref[...] = x_ref[...] + _dot(merged.astype(BF16), wo_ref[...])


def _merge(ha, ao, gs, x2, wpa, wpb, wo, tm):
    T = x2.shape[0]
    row = lambda w: pl.BlockSpec((tm, w), lambda i: (i, 0))
    return pl.pallas_call(
        _merge_kernel,
        grid=(T // tm,),
        in_specs=[row(1024), row(1024), row(2048), row(1024)] + [_const_spec(w.shape) for w in (wpa, wpb, wo)],
        out_specs=row(1024),
        out_shape=jax.ShapeDtypeStruct((T, D_MODEL), F32),
        compiler_params=_cparams("parallel"),
        name="merge",
    )(ha, ao, gs, x2, wpa, wpb, wo)


N_EXTRACT = PEER_TOPK + 1


def _top_rows(s, n):
    rows = []
    for _ in range(n):
        m = jnp.max(s, axis=0, keepdims=True)
        rows.append(m)
        s = jnp.where(s == m, -jnp.inf, s)
    return rows


def _stack8(rows, t):
    ri = lax.broadcasted_iota(jnp.int32, (8, t), 0)
    out = jnp.full((8, t), -jnp.inf, F32)
    for k, r in enumerate(rows):
        out = jnp.where(ri == k, r, out)
    return out


def _route_head(h, q_scr, k1, k2, p1_ref, p2_ref, th_ref):
    qh = q_scr[h]
    t = qh.shape[0]
    s1 = _dot_nt(k1, qh[:, :PEER_HALF])
    s2 = _dot_nt(k2, qh[:, PEER_HALF:])
    v1 = _top_rows(s1, N_EXTRACT)
    v2 = _top_rows(s2, N_EXTRACT)
    v2a, v2b, v2c = _stack8(v2[0:8], t), _stack8(v2[8:16], t), _stack8(v2[16:17], t)
    v1b, v1c = _stack8(v1[8:16], t), _stack8(v1[16:17], t)
    cand = jnp.concatenate([v1[i] + v2a for i in range(8)] +
                           [v1[0] + v2b, v1[0] + v2c, v1b + v2[0], v1c + v2[0]], axis=0)
    best = _top_rows(cand, N_EXTRACT)
    z = jnp.zeros_like(best[0])
    for b in best[:PEER_TOPK]:
        z = z + jnp.exp(b - best[0])
    theta = 0.5 * (best[PEER_TOPK - 1] + best[PEER_TOPK])
    m1, m2 = v1[0], v2[0]
    p1_ref[h] = jnp.exp(s1 - m1) / z
    p2_ref[h] = jnp.exp(s2 - m2)
    th_ref[pl.ds(h, 1), :] = jnp.exp(theta - m1 - m2) / z


def _route_kernel(x_ref, g_ref, wq_ref, k1_ref, k2_ref, xt_ref, p1_ref, p2_ref, th_ref, q_scr):
    xn = _rms(x_ref[...], g_ref[...])
    xt_ref[...] = xn.T.astype(BF16)
    q = _dot(xn.astype(BF16), wq_ref[...])
    for h in range(PEER_HEADS):
        q_scr[h] = q[:, h * PEER_DQ:(h + 1) * PEER_DQ].astype(BF16)
    k1, k2 = k1_ref[...], k2_ref[...]

    def head_pair(i, carry):
        _route_head(2 * i, q_scr, k1, k2, p1_ref, p2_ref, th_ref)
        _route_head(2 * i + 1, q_scr, k1, k2, p1_ref, p2_ref, th_ref)
        return carry

    lax.fori_loop(0, PEER_HEADS // 2, head_pair, 0)


def _route(x1, g, wq, k1, k2, tr):
    T = x1.shape[0]
    return pl.pallas_call(
        _route_kernel,
        grid=(T // tr,),
        in_specs=[pl.BlockSpec((tr, D_MODEL), lambda i: (i, 0)), _const_spec(g.shape), _const_spec(wq.shape),
                  _const_spec(k1.shape), _const_spec(k2.shape)],
        out_specs=[pl.BlockSpec((D_MODEL, tr), lambda i: (0, i)),
                   pl.BlockSpec((PEER_HEADS, PEER_NKEYS, tr), lambda i: (0, 0, i)),
                   pl.BlockSpec((PEER_HEADS, PEER_NKEYS, tr), lambda i: (0, 0, i)),
                   pl.BlockSpec((PEER_HEADS, tr), lambda i: (0, i))],
        out_shape=[jax.ShapeDtypeStruct((D_MODEL, T), BF16),
                   jax.ShapeDtypeStruct((PEER_HEADS, PEER_NKEYS, T), F32),
                   jax.ShapeDtypeStruct((PEER_HEADS, PEER_NKEYS, T), F32),
                   jax.ShapeDtypeStruct((PEER_HEADS, T), F32)],
        scratch_shapes=[pltpu.VMEM((PEER_HEADS, tr, PEER_DQ), BF16)],
        compiler_params=_cparams("parallel"),
        name="route",
    )(x1, g, wq, k1, k2)


E1_PER_STEP = 8
RSQRT2 = 0.7071067811865476


def _peer_kernel(xt_ref, u_ref, vt_ref, p1_ref, p2_ref, th_ref, x1_ref, o_ref, acc_scr, g_scr):
    s = pl.program_id(0)
    nblk = PEER_EXPERTS // (E1_PER_STEP * PEER_NKEYS)

    @pl.when(s == 0)
    def _():
        acc_scr[...] = jnp.zeros_like(acc_scr)
        g_scr[...] = jnp.zeros_like(g_scr)

    @pl.when((lax.rem(s, nblk) == 1) & (s > 1))
    def _():
        o_ref[...] = x1_ref[...] + acc_scr[...].T
        acc_scr[...] = jnp.zeros_like(acc_scr)

    slot = lax.rem(s, 2)
    acc_scr[...] += _dot(vt_ref[...], g_scr[1 - slot])
    xt = xt_ref[...]
    for e in range(E1_PER_STEP):
        sl = slice(e * PEER_NKEYS, (e + 1) * PEER_NKEYS)
        a = _dot(u_ref[sl, :], xt)
        act = 0.5 * a * (1.0 + lax.erf(a * RSQRT2))
        w = jnp.zeros_like(a)
        for h in range(PEER_HEADS):
            p = p2_ref[h] * p1_ref[h, e:e + 1, :]
            w = w + jnp.where(p >= th_ref[h:h + 1, :], p, 0.0)
        g_scr[slot, sl, :] = (w * act).astype(BF16)


def _peer(xt, u, vt, p1, p2, th, x1, tb):
    T = x1.shape[0]
    eb = E1_PER_STEP * PEER_NKEYS
    nblk = PEER_EXPERTS // eb
    nt = T // tb
    cur = lambda s: jnp.minimum(s // nblk, nt - 1)
    done = lambda s: jnp.maximum(s - 2, 0) // nblk
    return pl.pallas_call(
        _peer_kernel,
        grid=(nt * nblk + 2,),
        in_specs=[pl.BlockSpec((D_MODEL, tb), lambda s: (0, cur(s))),
                  pl.BlockSpec((eb, D_MODEL), lambda s: (s % nblk, 0)),
                  pl.BlockSpec((D_MODEL, eb), lambda s: (0, (s + nblk - 1) % nblk)),
                  pl.BlockSpec((PEER_HEADS, E1_PER_STEP, tb), lambda s: (0, s % nblk, cur(s))),
                  pl.BlockSpec((PEER_HEADS, PEER_NKEYS, tb), lambda s: (0, 0, cur(s))),
                  pl.BlockSpec((PEER_HEADS, tb), lambda s: (0, cur(s))),
                  pl.BlockSpec((tb, D_MODEL), lambda s: (done(s), 0))],
        out_specs=pl.BlockSpec((tb, D_MODEL), lambda s: (done(s), 0)),
        out_shape=jax.ShapeDtypeStruct((T, D_MODEL), F32),
        scratch_shapes=[pltpu.VMEM((D_MODEL, tb), F32), pltpu.VMEM((2, eb, tb), BF16)],
        compiler_params=_cparams("arbitrary"),
        name="peer",
    )(xt, u, vt, p1, p2, th, x1)


def _prep_weights(p):
    w_in = p["w_in"]
    sizes = (A_HEADS * A_DQK, A_HEADS * A_DQK, A_HEADS * A_DV, A_HEADS * A_DV, 4 * A_HEADS,
             Q_LORA, KV_LORA, B_ROPE, 2 * D_MODEL)
    pts = np.cumsum((0,) + sizes)
    w_q, w_k, w_v, w_o, w_gate, w_cq, w_ckv, w_kr, w_gbr = (w_in[:, pts[i]:pts[i + 1]] for i in range(9))

    def pad_heads(w):
        w = w.reshape(D_MODEL, A_HEADS, A_DQK)
        return jnp.pad(w, ((0, 0), (0, 0), (0, LANE - A_DQK))).reshape(D_MODEL, A_HEADS * LANE)

    out = {}
    out["wa"] = jnp.concatenate([pad_heads(w_q) * (A_DQK ** -0.5), w_v, w_o], axis=1).astype(BF16)
    out["wc"] = jnp.concatenate([w_cq, w_ckv], axis=1).astype(BF16)
    zpad = jnp.zeros((D_MODEL, LANE - B_ROPE - 4 * A_HEADS), F32)
    out["wkg"] = jnp.concatenate([w_kr, w_gate, zpad], axis=1).astype(BF16)
    out["wg"] = w_gbr.astype(BF16)
    out["wkt"] = pad_heads(w_k).T.astype(BF16)
    out["wgt"] = w_gate.T.astype(BF16)
    bias = p["b_mgate"].astype(F32).reshape(4 * A_HEADS)
    out["bkg"] = jnp.concatenate([jnp.zeros((B_ROPE,), F32), bias, jnp.zeros((32,), F32)]).reshape(1, LANE)
    out["bgt"] = bias.reshape(4 * A_HEADS, 1)
    out["g_mix"] = p["norm_mix_g"].astype(F32).reshape(1, D_MODEL)
    out["g_mlstm"] = p["mlstm_norm_g"].astype(F32).reshape(1, A_HEADS * A_DV)
    out["gq"] = p["q_a_norm_g"].astype(F32).reshape(1, Q_LORA)
    out["gkv"] = p["kv_a_norm_g"].astype(F32).reshape(1, KV_LORA)
    wq = p["w_q_up"].reshape(Q_LORA, B_HEADS, B_DQK).transpose(1, 0, 2)
    out["wq"] = jnp.pad(wq, ((0, 0), (0, 0), (0, QK_PAD - B_DQK))).astype(BF16)
    wkv = p["w_kv_up"].reshape(KV_LORA, B_HEADS, B_NOPE + B_DV).transpose(1, 0, 2)
    out["wkbt"] = wkv[:, :, :B_NOPE].transpose(0, 2, 1).astype(BF16)
    out["wv"] = wkv[:, :, B_NOPE:].astype(BF16)
    out["nq"] = jnp.pad(p["qk_norm_q_g"].astype(F32), (0, QK_PAD - B_DQK)).reshape(1, QK_PAD)
    nk = p["qk_norm_k_g"].astype(F32)
    out["nk_col"] = jnp.pad(nk, (0, QK_PAD - B_DQK)).reshape(QK_PAD, 1)
    out["kb"] = (1.01 * B_DQK ** 0.5 * jnp.max(jnp.abs(nk))).reshape(1, 1)
    out["score_bound"] = 1.01 * B_DQK ** 0.5 * LOG2E * jnp.max(jnp.abs(nk)) * jnp.max(jnp.abs(p["qk_norm_q_g"]))
    out["wpa"] = p["w_proj_a"].astype(BF16)
    out["wpb"] = p["w_proj_b"].astype(BF16)
    out["wo"] = p["w_out"].astype(BF16)
    out["g_ffn"] = p["norm_ffn_g"].astype(F32).reshape(1, D_MODEL)
    out["wpq"] = p["w_peer_q"].astype(BF16)
    out["k1"] = p["peer_keys1"].astype(BF16)
    out["k2"] = p["peer_keys2"].astype(BF16)
    out["u"] = p["peer_u"].astype(BF16)
    out["vt"] = p["peer_v"].T.astype(BF16)
    return out


def _rope_tables(S):
    pos = jnp.arange(S, dtype=F32)
    inv = ROPE_THETA ** (-jnp.arange(0, B_ROPE, 2, dtype=F32) / B_ROPE)
    ang = pos[:, None] * inv[None, :]
    ang = jnp.concatenate([ang, ang], axis=-1)
    sign = jnp.where(jnp.arange(B_ROPE) < B_ROPE // 2, -1.0, 1.0).astype(F32)
    pad = ((0, 0), (0, LANE - B_ROPE))
    cos, sin = jnp.pad(jnp.cos(ang), pad), jnp.pad(jnp.sin(ang) * sign, pad)
    return cos, sin, cos.T, sin.T


def _pick(n, pref):
    t = min(n, pref)
    assert n % t == 0, (n, t)
    return t


def _layer(x, w):
    B, S, _ = x.shape
    T = B * S
    x2 = x.reshape(T, D_MODEL)
    qa, va, osig, cqkv, kg, gs, kt, gt = _inproj(
        x2, w["g_mix"], w["wa"], w["wc"], w["wkg"], w["wg"], w["wkt"], w["wgt"], w["bkg"], w["bgt"],
        _pick(T, 256))
    hf = _mlstm(qa, kt, va, kg, gt, B, S, False)
    ha = _mlstm(qa, kt, va, kg, gt, B, S, True, hf, osig, w["g_mlstm"])
    cos, sin, cost, sint = _rope_tables(S)
    ts = _pick(S, 256)
    nkt = jnp.broadcast_to(w["nk_col"], (QK_PAD, ts))
    q, kbt, v = _mlaprep(cqkv, kg, cos, sin, cost, sint, w["gq"], w["gkv"], w["wq"], w["wkbt"], w["wv"], w["nq"],
                         nkt, w["kb"], B, S, ts)
    tq, tk = _pick(S, 512), _pick(S, 512)
    ao = lax.cond(w["score_bound"] <= 50.0,
                  lambda *a: _attention(*a, tq, tk, True),
                  lambda *a: _attention(*a, tq, tk, False),
                  q, kbt, v).reshape(T, B_HEADS * B_DV)
    x1 = _merge(ha, ao, gs, x2, w["wpa"], w["wpb"], w["wo"], _pick(T, 512))
    xt, p1, p2, th = _route(x1, w["g_ffn"], w["wpq"], w["k1"], w["k2"], _pick(T, 256))
    y = _peer(xt, w["u"], w["vt"], p1, p2, th, x1, _pick(T, 256))
    return y.reshape(B, S, D_MODEL)


def kernel(x_prompt, x_sample, norm_mix_g, w_in, b_mgate, mlstm_norm_g, q_a_norm_g, w_q_up, kv_a_norm_g, w_kv_up, qk_norm_q_g, qk_norm_k_g, w_proj_a, w_proj_b, w_out, norm_ffn_g, w_peer_q, peer_keys1, peer_keys2, peer_u, peer_v):
    params = dict(norm_mix_g=norm_mix_g, w_in=w_in, b_mgate=b_mgate, mlstm_norm_g=mlstm_norm_g,
                  q_a_norm_g=q_a_norm_g, w_q_up=w_q_up, kv_a_norm_g=kv_a_norm_g, w_kv_up=w_kv_up,
                  qk_norm_q_g=qk_norm_q_g, qk_norm_k_g=qk_norm_k_g, w_proj_a=w_proj_a, w_proj_b=w_proj_b,
                  w_out=w_out, norm_ffn_g=norm_ffn_g, w_peer_q=w_peer_q, peer_keys1=peer_keys1,
                  peer_keys2=peer_keys2, peer_u=peer_u, peer_v=peer_v)
    depth = w_in.shape[0]
    layers = [_prep_weights({k: v[l] for k, v in params.items()}) for l in range(depth)]

    def run(x):
        for w in layers:
            x = _layer(x, w)
        return x

    return (run(x_prompt), run(x_sample))
```

```python
import functools

import jax
import jax.numpy as jnp
import numpy as np
from jax import lax
from jax.experimental import pallas as pl
from jax.experimental.pallas import tpu as pltpu

F32 = jnp.float32
BF16 = jnp.bfloat16

D_MODEL = 1024
A_HEADS, A_DQK, A_DV, A_CHUNK = 8, 64, 128, 128
B_HEADS, B_NOPE, B_ROPE, B_DV = 8, 128, 64, 128
B_DQK = B_NOPE + B_ROPE
Q_LORA, KV_LORA = 384, 256
ROPE_THETA = 10000.0
PEER_HEADS, PEER_NKEYS, PEER_DQ, PEER_TOPK = 8, 128, 256, 16
PEER_HALF = PEER_DQ // 2
PEER_EXPERTS = PEER_NKEYS * PEER_NKEYS
EPS = 1e-6

LANE = 128
QK_PAD = 2 * LANE
VMEM_LIMIT = 52 * 1024 * 1024


def _cparams(*sem):
    return pltpu.CompilerParams(dimension_semantics=sem, vmem_limit_bytes=VMEM_LIMIT)


def _dot(a, b):
    return jnp.dot(a, b, preferred_element_type=F32)


def _dot_nt(a, b):
    return lax.dot_general(a, b, (((1,), (1,)), ((), ())), preferred_element_type=F32)


def _rms(x, g):
    return x * lax.rsqrt(jnp.mean(x * x, axis=-1, keepdims=True) + EPS) * g


def _log_sigmoid(x):
    return jnp.minimum(x, 0.0) - jnp.log(1.0 + jnp.exp(-jnp.abs(x)))


def _sigmoid(x):
    return 1.0 / (1.0 + jnp.exp(-x))


def _const_spec(shape):
    nd = len(shape)
    return pl.BlockSpec(shape, lambda *_: (0,) * nd)


def _inproj_kernel(x_ref, g_ref, wa_ref, wc_ref, wkg_ref, wg_ref, wkt_ref, wgt_ref, bkg_ref, bgt_ref,
                   qa_ref, va_ref, os_ref, cqkv_ref, kg_ref, gs_ref, kt_ref, gt_ref):
    h = _rms(x_ref[...], g_ref[...]).astype(BF16)
    a = _dot(h, wa_ref[...])
    qa_ref[...] = a[:, :1024].astype(BF16)
    va_ref[...] = a[:, 1024:2048].astype(BF16)
    os_ref[...] = _sigmoid(a[:, 2048:]).astype(BF16)
    cqkv_ref[...] = _dot(h, wc_ref[...])
    kg = _dot(h, wkg_ref[...]) + bkg_ref[...]
    lane = lax.broadcasted_iota(jnp.int32, kg.shape, 1)
    is_f = ((lane >= 72) & (lane < 80)) | (lane >= 88)
    kg_ref[...] = jnp.where(is_f, _log_sigmoid(kg), kg)
    gs_ref[...] = _sigmoid(_dot(h, wg_ref[...])).astype(BF16)
    kt_ref[...] = _dot_nt(wkt_ref[...], h).astype(BF16)
    gt = _dot_nt(wgt_ref[...], h) + bgt_ref[...]
    row = lax.broadcasted_iota(jnp.int32, gt.shape, 0)
    is_fr = ((row >= 8) & (row < 16)) | (row >= 24)
    gt_ref[...] = jnp.where(is_fr, _log_sigmoid(gt), gt)


def _inproj(x2, g, wa, wc, wkg, wg, wkt, wgt, bkg, bgt, tm):
    T = x2.shape[0]
    row = lambda w: pl.BlockSpec((tm, w), lambda i: (i, 0))
    col = lambda r: pl.BlockSpec((r, tm), lambda i: (0, i))
    return pl.pallas_call(
        _inproj_kernel,
        grid=(T // tm,),
        in_specs=[row(D_MODEL), _const_spec(g.shape), _const_spec(wa.shape), _const_spec(wc.shape),
                  _const_spec(wkg.shape), _const_spec(wg.shape), _const_spec(wkt.shape),
                  _const_spec(wgt.shape), _const_spec(bkg.shape), _const_spec(bgt.shape)],
        out_specs=[row(1024), row(1024), row(1024), row(640), row(128), row(2048), col(1024), col(32)],
        out_shape=[jax.ShapeDtypeStruct((T, 1024), BF16), jax.ShapeDtypeStruct((T, 1024), BF16),
                   jax.ShapeDtypeStruct((T, 1024), BF16), jax.ShapeDtypeStruct((T, 640), F32),
                   jax.ShapeDtypeStruct((T, 128), F32), jax.ShapeDtypeStruct((T, 2048), BF16),
                   jax.ShapeDtypeStruct((1024, T), BF16), jax.ShapeDtypeStruct((32, T), F32)],
        compiler_params=_cparams("parallel"),
        name="inproj",
    )(x2, g, wa, wc, wkg, wg, wkt, wgt, bkg, bgt)


def _split3(x):
    hi = x.astype(BF16)
    r = x - hi.astype(F32)
    mid = r.astype(BF16)
    lo = (r - mid.astype(F32)).astype(BF16)
    return hi, mid, lo


def _dot3(a_f32, b_bf16):
    hi, mid, lo = _split3(a_f32)
    return _dot(hi, b_bf16) + _dot(mid, b_bf16) + _dot(lo, b_bf16)


def _dot3r(a_bf16, b_f32):
    hi, mid, lo = _split3(b_f32)
    return _dot(a_bf16, hi) + _dot(a_bf16, mid) + _dot(a_bf16, lo)


def _cummax_rows(y, reverse):
    n = y.shape[0]
    row = lax.broadcasted_iota(jnp.int32, y.shape, 0)
    k = 1
    while k < n:
        if reverse:
            sh = jnp.where(row < n - k, pltpu.roll(y, n - k, 0), -jnp.inf)
        else:
            sh = jnp.where(row >= k, pltpu.roll(y, k, 0), -jnp.inf)
        y = jnp.maximum(y, sh)
        k *= 2
    return y


def _mlstm_kernel(*refs, reverse):
    if reverse:
        q_ref, kt_ref, v_ref, kg_ref, gt_ref, hf_ref, os_ref, ng_ref, out_ref, cn_scr, m_scr = refs
    else:
        q_ref, kt_ref, v_ref, kg_ref, gt_ref, out_ref, cn_scr, m_scr = refs
    L = A_CHUNK

    @pl.when(pl.program_id(1) == 0)
    def _():
        cn_scr[...] = jnp.zeros_like(cn_scr)
        m_scr[...] = jnp.zeros_like(m_scr)

    row = lax.broadcasted_iota(jnp.int32, (L, L), 0)
    col = lax.broadcasted_iota(jnp.int32, (L, L), 1)
    mask = (col >= row) if reverse else (col <= row)
    tri_c = jnp.where(mask, 1.0, 0.0).astype(BF16)
    tri_r = jnp.where((row >= col) if reverse else (row <= col), 1.0, 0.0).astype(BF16)

    g = kg_ref[:, 64:96]
    gal = kg_ref[:, 96:128]
    gt = gt_ref[...]
    b_al = _dot3r(tri_c, gal)
    b_rows = _dot3(gt, tri_r)
    y = jnp.concatenate([g - b_al, jnp.zeros((L, LANE - 4 * A_HEADS), F32)], axis=1)
    cm = _cummax_rows(y, reverse)[:, :4 * A_HEADS]
    bc3 = jnp.concatenate(_split3(jnp.concatenate([b_al, cm], axis=1)), axis=1)
    off = 16 if reverse else 0
    ones = jnp.ones((L, LANE), BF16)
    sel_r = lax.broadcasted_iota(jnp.int32, (6 * 4 * A_HEADS, 2 * LANE), 0)
    sel_c = lax.broadcasted_iota(jnp.int32, (6 * 4 * A_HEADS, 2 * LANE), 1)
    sel_r = sel_r % (8 * A_HEADS) - jnp.where(sel_c >= LANE, 4 * A_HEADS, 0)
    last = 0 if reverse else L - 1

    H = range(A_HEADS)
    sls = [slice(h * LANE, (h + 1) * LANE) for h in H]
    bc = [_dot(bc3, jnp.where(sel_r == off + h, 1.0, 0.0).astype(BF16)) for h in H]
    b_c = [x[:, :LANE] for x in bc]
    cm_c = [x[:, LANE:] for x in bc]
    qk = [_dot(q_ref[:, sls[h]], kt_ref[sls[h], :]) for h in H]
    cns = [cn_scr[h] for h in H]
    qc = [_dot(q_ref[:, sls[h]], cns[h].astype(BF16)) for h in H]
    ms = [m_scr[h] for h in H]
    i_r = [gt[off + h:off + h + 1, :] for h in H]
    b_r = [b_rows[off + 8 + h:off + 9 + h, :] for h in H]
    m_t = [b_c[h] + jnp.maximum(cm_c[h], ms[h]) for h in H]
    s = [(qk[h] * jnp.where(mask, jnp.exp(b_c[h] - b_r[h] + i_r[h] - m_t[h]), 0.0)).astype(BF16) for h in H]
    vext = [jnp.concatenate([v_ref[:, sls[h]], ones], axis=1) for h in H]
    sv = [_dot(s[h], vext[h]) for h in H]
    b_last = [b_c[h][last:last + 1] for h in H]
    m_new = [b_last[h] + jnp.maximum(ms[h], cm_c[h][last:last + 1]) for h in H]
    kts = [(kt_ref[sls[h], :].astype(F32) * jnp.exp(b_last[h] - b_r[h] + i_r[h] - m_new[h])).astype(BF16) for h in H]
    kv = [_dot(kts[h], vext[h]) for h in H]
    for h in H:
        decay = jnp.exp(b_last[h] + ms[h] - m_new[h])
        cn_scr[h] = jnp.concatenate([decay, decay], axis=1) * cns[h] + kv[h]
        m_scr[h] = m_new[h]
        inter_w = jnp.exp(b_c[h] + ms[h] - m_t[h])
        tot = sv[h] + jnp.concatenate([inter_w, inter_w], axis=1) * qc[h]
        hh = tot[:, :LANE] / jnp.maximum(jnp.abs(tot[:, LANE:]), jnp.exp(-m_t[h]))
        if reverse:
            t = hh + hf_ref[:, sls[h]]
            y = t * lax.rsqrt(jnp.mean(t * t, axis=-1, keepdims=True) + EPS) * ng_ref[:, sls[h]]
            out_ref[:, sls[h]] = (y * os_ref[:, sls[h]].astype(F32)).astype(BF16)
        else:
            out_ref[:, sls[h]] = hh


def _mlstm(qa, kt, va, kg, gt, B, S, reverse, hf=None, osig=None, ng=None):
    L = A_CHUNK
    nc = S // L
    T = B * S
    if reverse:
        blk = lambda b, c: b * nc + (nc - 1 - c)
    else:
        blk = lambda b, c: b * nc + c
    row = lambda w: pl.BlockSpec((L, w), lambda b, c: (blk(b, c), 0))
    col = lambda r: pl.BlockSpec((r, L), lambda b, c: (0, blk(b, c)))
    in_specs = [row(1024), col(1024), row(1024), row(128), col(32)]
    args = [qa, kt, va, kg, gt]
    if reverse:
        in_specs += [row(1024), row(1024), _const_spec(ng.shape)]
        args += [hf, osig, ng]
    return pl.pallas_call(
        functools.partial(_mlstm_kernel, reverse=reverse),
        grid=(B, nc),
        in_specs=in_specs,
        out_specs=row(1024),
        out_shape=jax.ShapeDtypeStruct((T, 1024), BF16 if reverse else F32),
        scratch_shapes=[pltpu.VMEM((A_HEADS, LANE, 2 * LANE), F32), pltpu.VMEM((A_HEADS, 1, LANE), F32)],
        compiler_params=_cparams("parallel", "arbitrary"),
        name="mlstm_bwd" if reverse else "mlstm_fwd",
    )(*args)


def _rope(x, cos, sin_signed, axis):
    idx = lax.broadcasted_iota(jnp.int32, x.shape, axis)
    half = B_ROPE // 2
    rot = jnp.where(idx < half, pltpu.roll(x, LANE - half, axis), pltpu.roll(x, half, axis))
    return x * cos + rot * sin_signed


BOUND_COL = B_DQK - LANE
LOG2E = 1.4426950408889634


def _mlaprep_kernel(cqkv_ref, kg_ref, cos_ref, sin_ref, cost_ref, sint_ref, gq_ref, gkv_ref, wq_ref, wkt_ref,
                    wv_ref, nq_ref, nkt_ref, kb_ref, q_ref, kt_ref, v_ref):
    ts = cqkv_ref.shape[0]
    cq = _rms(cqkv_ref[:, :Q_LORA], gq_ref[...]).astype(BF16)
    ckv = _rms(cqkv_ref[:, Q_LORA:], gkv_ref[...]).astype(BF16)
    kgt = kg_ref[...].T
    rowi = lax.broadcasted_iota(jnp.int32, kgt.shape, 0)
    krt = jnp.where(rowi < B_ROPE, kgt, 0.0)
    ss_r = jnp.sum(krt * krt, axis=0, keepdims=True)
    nkt = nkt_ref[...]
    krt_rot = _rope(krt * nkt[LANE:], cost_ref[...], sint_ref[...], 0)
    one_row = jnp.where(rowi == BOUND_COL, 1.0, 0.0)
    nq = nq_ref[...]
    cos, sin = cos_ref[...], sin_ref[...]
    lane = lax.broadcasted_iota(jnp.int32, (ts, LANE), 1)
    scale = B_DQK ** -0.5 * LOG2E
    kb = kb_ref[...]
    ones = jnp.ones((ts, B_DV), BF16)
    for h in range(B_HEADS):
        q = _dot(cq, wq_ref[h])
        rq = lax.rsqrt(jnp.sum(q * q, axis=-1, keepdims=True) * (1.0 / B_DQK) + EPS) * scale
        qn = q * nq
        qnorm = jnp.sqrt(jnp.sum(qn * qn, axis=-1, keepdims=True)) * rq
        q_ref[0, h, :, :LANE] = (qn[:, :LANE] * rq).astype(BF16)
        q_rot = _rope(qn[:, LANE:], cos, sin, 1) * rq
        q_ref[0, h, :, LANE:] = jnp.where(lane == BOUND_COL, -(qnorm * kb), q_rot).astype(BF16)
        knt = _dot_nt(wkt_ref[h], ckv)
        rk = lax.rsqrt((jnp.sum(knt * knt, axis=0, keepdims=True) + ss_r) * (1.0 / B_DQK) + EPS)
        kt_ref[0, h, :LANE, :] = (knt * nkt[:LANE] * rk).astype(BF16)
        kt_ref[0, h, LANE:, :] = (krt_rot * rk + one_row).astype(BF16)
        v_ref[0, h, :, :B_DV] = _dot(ckv, wv_ref[h]).astype(BF16)
        v_ref[0, h, :, B_DV:] = ones


def _mlaprep(cqkv, kg, cos, sin, cost, sint, gq, gkv, wq, wkt, wv, nq, nkt, kb, B, S, ts):
    ns = S // ts
    row = lambda w: pl.BlockSpec((ts, w), lambda b, i: (b * ns + i, 0))
    pos = pl.BlockSpec((ts, LANE), lambda b, i: (i, 0))
    post = pl.BlockSpec((LANE, ts), lambda b, i: (0, i))
    consts = (gq, gkv, wq, wkt, wv, nq, nkt, kb)
    return pl.pallas_call(
        _mlaprep_kernel,
        grid=(B, ns),
        in_specs=[row(640), row(128), pos, pos, post, post] + [_const_spec(a.shape) for a in consts],
        out_specs=[pl.BlockSpec((1, B_HEADS, ts, QK_PAD), lambda b, i: (b, 0, i, 0)),
                   pl.BlockSpec((1, B_HEADS, QK_PAD, ts), lambda b, i: (b, 0, 0, i)),
                   pl.BlockSpec((1, B_HEADS, ts, 2 * B_DV), lambda b, i: (b, 0, i, 0))],
        out_shape=[jax.ShapeDtypeStruct((B, B_HEADS, S, QK_PAD), BF16),
                   jax.ShapeDtypeStruct((B, B_HEADS, QK_PAD, S), BF16),
                   jax.ShapeDtypeStruct((B, B_HEADS, S, 2 * B_DV), BF16)],
        compiler_params=_cparams("parallel", "parallel"),
        name="mlaprep",
    )(cqkv, kg, cos, sin, cost, sint, *consts)


def _attn_kernel(q_ref, kt_ref, v_ref, o_ref, *, tk, nk):
    q = q_ref[0, 0]
    tq = q.shape[0]

    def body(j, carry):
        m, acc = carry
        start = pl.multiple_of(j * tk, tk)
        kt = kt_ref[0, 0, :, pl.ds(start, tk)]
        v = v_ref[0, 0, pl.ds(start, tk), :]
        s = _dot(q, kt)
        m_new = jnp.maximum(m, jnp.max(s, axis=1, keepdims=True))
        p = jnp.exp2(s - m_new)
        acc = jnp.exp2(m - m_new) * acc + _dot(p.astype(BF16), v)
        return m_new, acc

    init = (jnp.full((tq, 1), -jnp.inf, F32), jnp.zeros((tq, 2 * B_DV), F32))
    _, acc = lax.fori_loop(0, nk, body, init)
    o_ref[0] = (acc[:, :B_DV] / acc[:, B_DV:]).astype(BF16)


def _attn_fast_kernel(q_ref, kt_ref, v_ref, o_ref, *, tk, nk, unroll):
    q = q_ref[0, 0]
    tq = q.shape[0]

    def probs(j):
        kt = kt_ref[0, 0, :, pl.ds(pl.multiple_of(j * tk, tk), tk)]
        return jnp.exp2(_dot(q, kt)).astype(BF16)

    def weighted(j, p, acc):
        v = v_ref[0, 0, pl.ds(pl.multiple_of(j * tk, tk), tk), :]
        return acc + _dot(p, v)

    def body(j, carry):
        p, acc = carry
        return probs(j), weighted(j - 1, p, acc)

    p, acc = lax.fori_loop(1, nk, body, (probs(0), jnp.zeros((tq, 2 * B_DV), F32)), unroll=unroll)
    acc = weighted(nk - 1, p, acc)
    o_ref[0] = (acc[:, :B_DV] / acc[:, B_DV:]).astype(BF16)


def _attention(q, kt, v, tq, tk, fast):
    B, H, S, _ = q.shape
    nk = S // tk
    if fast:
        body, name = functools.partial(_attn_fast_kernel, tk=tk, nk=nk, unroll=max(1, min(8, nk - 1))), "attention_fast"
    else:
        body, name = functools.partial(_attn_kernel, tk=tk, nk=nk), "attention"
    return pl.pallas_call(
        body,
        grid=(B, H, S // tq),
        in_specs=[pl.BlockSpec((1, 1, tq, QK_PAD), lambda b, h, i: (b, h, i, 0)),
                  pl.BlockSpec((1, 1, QK_PAD, S), lambda b, h, i: (b, h, 0, 0)),
                  pl.BlockSpec((1, 1, S, 2 * B_DV), lambda b, h, i: (b, h, 0, 0))],
        out_specs=pl.BlockSpec((1, tq, B_DV), lambda b, h, i: (b, i, h)),
        out_shape=jax.ShapeDtypeStruct((B, S, H * B_DV), BF16),
        compiler_params=_cparams("parallel", "parallel", "arbitrary"),
        name=name,
    )(q, kt, v)


def _merge_kernel(ha_ref, ao_ref, gs_ref, x_ref, wpa_ref, wpb_ref, wo_ref, o_ref):
    ya = _dot(ha_ref[...], wpa_ref[...])
    yb = _dot(ao_ref[...], wpb_ref[...])
    merged = gs_ref[:, :D_MODEL].astype(F32) * ya + gs_ref[:, D_MODEL:].astype(F32) * yb
    o_ref[...] = x_ref[...] + _dot(merged.astype(BF16), wo_ref[...])


def _merge(ha, ao, gs, x2, wpa, wpb, wo, tm):
    T = x2.shape[0]
    row = lambda w: pl.BlockSpec((tm, w), lambda i: (i, 0))
    return pl.pallas_call(
        _merge_kernel,
        grid=(T // tm,),
        in_specs=[row(1024), row(1024), row(2048), row(1024)] + [_const_spec(w.shape) for w in (wpa, wpb, wo)],
        out_specs=row(1024),
        out_shape=jax.ShapeDtypeStruct((T, D_MODEL), F32),
        compiler_params=_cparams("parallel"),
        name="merge",
    )(ha, ao, gs, x2, wpa, wpb, wo)


N_EXTRACT = PEER_TOPK + 1


def _top_rows(s, n):
    rows = []
    for _ in range(n):
        m = jnp.max(s, axis=0, keepdims=True)
        rows.append(m)
        s = jnp.where(s == m, -jnp.inf, s)
    return rows


def _stack8(rows, t):
    ri = lax.broadcasted_iota(jnp.int32, (8, t), 0)
    out = jnp.full((8, t), -jnp.inf, F32)
    for k, r in enumerate(rows):
        out = jnp.where(ri == k, r, out)
    return out


def _route_head(h, q_scr, k1, k2, p1_ref, p2_ref, th_ref):
    qh = q_scr[h]
    t = qh.shape[0]
    s1 = _dot_nt(k1, qh[:, :PEER_HALF])
    s2 = _dot_nt(k2, qh[:, PEER_HALF:])
    v1 = _top_rows(s1, N_EXTRACT)
    v2 = _top_rows(s2, N_EXTRACT)
    v2a, v2b, v2c = _stack8(v2[0:8], t), _stack8(v2[8:16], t), _stack8(v2[16:17], t)
    v1b, v1c = _stack8(v1[8:16], t), _stack8(v1[16:17], t)
    cand = jnp.concatenate([v1[i] + v2a for i in range(8)] +
                           [v1[0] + v2b, v1[0] + v2c, v1b + v2[0], v1c + v2[0]], axis=0)
    best = _top_rows(cand, N_EXTRACT)
    z = jnp.zeros_like(best[0])
    for b in best[:PEER_TOPK]:
        z = z + jnp.exp(b - best[0])
    theta = 0.5 * (best[PEER_TOPK - 1] + best[PEER_TOPK])
    m1, m2 = v1[0], v2[0]
    p1_ref[h] = jnp.exp(s1 - m1) / z
    p2_ref[h] = jnp.exp(s2 - m2)
    th_ref[pl.ds(h, 1), :] = jnp.exp(theta - m1 - m2) / z


def _route_kernel(x_ref, g_ref, wq_ref, k1_ref, k2_ref, xt_ref, p1_ref, p2_ref, th_ref, q_scr):
    xn = _rms(x_ref[...], g_ref[...])
    xt_ref[...] = xn.T.astype(BF16)
    q = _dot(xn.astype(BF16), wq_ref[...])
    for h in range(PEER_HEADS):
        q_scr[h] = q[:, h * PEER_DQ:(h + 1) * PEER_DQ].astype(BF16)
    k1, k2 = k1_ref[...], k2_ref[...]

    def head_pair(i, carry):
        _route_head(2 * i, q_scr, k1, k2, p1_ref, p2_ref, th_ref)
        _route_head(2 * i + 1, q_scr, k1, k2, p1_ref, p2_ref, th_ref)
        return carry

    lax.fori_loop(0, PEER_HEADS // 2, head_pair, 0)


def _route(x1, g, wq, k1, k2, tr):
    T = x1.shape[0]
    return pl.pallas_call(
        _route_kernel,
        grid=(T // tr,),
        in_specs=[pl.BlockSpec((tr, D_MODEL), lambda i: (i, 0)), _const_spec(g.shape), _const_spec(wq.shape),
                  _const_spec(k1.shape), _const_spec(k2.shape)],
        out_specs=[pl.BlockSpec((D_MODEL, tr), lambda i: (0, i)),
                   pl.BlockSpec((PEER_HEADS, PEER_NKEYS, tr), lambda i: (0, 0, i)),
                   pl.BlockSpec((PEER_HEADS, PEER_NKEYS, tr), lambda i: (0, 0, i)),
                   pl.BlockSpec((PEER_HEADS, tr), lambda i: (0, i))],
        out_shape=[jax.ShapeDtypeStruct((D_MODEL, T), BF16),
                   jax.ShapeDtypeStruct((PEER_HEADS, PEER_NKEYS, T), F32),
                   jax.ShapeDtypeStruct((PEER_HEADS, PEER_NKEYS, T), F32),
                   jax.ShapeDtypeStruct((PEER_HEADS, T), F32)],
        scratch_shapes=[pltpu.VMEM((PEER_HEADS, tr, PEER_DQ), BF16)],
        compiler_params=_cparams("parallel"),
        name="route",
    )(x1, g, wq, k1, k2)


E1_PER_STEP = 16
RSQRT2 = 0.7071067811865476


def _peer_kernel(xt_ref, u_ref, vt_ref, p1_ref, p2_ref, th_ref, x1_ref, o_ref, acc_scr, g_scr):
    s = pl.program_id(0)
    nblk = PEER_EXPERTS // (E1_PER_STEP * PEER_NKEYS)

    @pl.when(s == 0)
    def _():
        acc_scr[...] = jnp.zeros_like(acc_scr)
        g_scr[...] = jnp.zeros_like(g_scr)

    @pl.when((lax.rem(s, nblk) == 1) & (s > 1))
    def _():
        o_ref[...] = x1_ref[...] + acc_scr[...].T
        acc_scr[...] = jnp.zeros_like(acc_scr)

    slot = lax.rem(s, 2)
    acc_scr[...] += _dot(vt_ref[...], g_scr[1 - slot])
    xt = xt_ref[...]
    for e in range(E1_PER_STEP):
        sl = slice(e * PEER_NKEYS, (e + 1) * PEER_NKEYS)
        a = _dot(u_ref[sl, :], xt)
        act = 0.5 * a * (1.0 + lax.erf(a * RSQRT2))
        w = jnp.zeros_like(a)
        for h in range(PEER_HEADS):
            p = p2_ref[h] * p1_ref[h, e:e + 1, :]
            w = w + jnp.where(p >= th_ref[h:h + 1, :], p, 0.0)
        g_scr[slot, sl, :] = (w * act).astype(BF16)


def _peer(xt, u, vt, p1, p2, th, x1, tb):
    T = x1.shape[0]
    eb = E1_PER_STEP * PEER_NKEYS
    nblk = PEER_EXPERTS // eb
    nt = T // tb
    cur = lambda s: jnp.minimum(s // nblk, nt - 1)
    done = lambda s: jnp.maximum(s - 2, 0) // nblk
    return pl.pallas_call(
        _peer_kernel,
        grid=(nt * nblk + 2,),
        in_specs=[pl.BlockSpec((D_MODEL, tb), lambda s: (0, cur(s))),
                  pl.BlockSpec((eb, D_MODEL), lambda s: (s % nblk, 0)),
                  pl.BlockSpec((D_MODEL, eb), lambda s: (0, (s + nblk - 1) % nblk)),
                  pl.BlockSpec((PEER_HEADS, E1_PER_STEP, tb), lambda s: (0, s % nblk, cur(s))),
                  pl.BlockSpec((PEER_HEADS, PEER_NKEYS, tb), lambda s: (0, 0, cur(s))),
                  pl.BlockSpec((PEER_HEADS, tb), lambda s: (0, cur(s))),
                  pl.BlockSpec((tb, D_MODEL), lambda s: (done(s), 0))],
        out_specs=pl.BlockSpec((tb, D_MODEL), lambda s: (done(s), 0)),
        out_shape=jax.ShapeDtypeStruct((T, D_MODEL), F32),
        scratch_shapes=[pltpu.VMEM((D_MODEL, tb), F32), pltpu.VMEM((2, eb, tb), BF16)],
        compiler_params=_cparams("arbitrary"),
        name="peer",
    )(xt, u, vt, p1, p2, th, x1)


def _prep_weights(p):
    w_in = p["w_in"]
    sizes = (A_HEADS * A_DQK, A_HEADS * A_DQK, A_HEADS * A_DV, A_HEADS * A_DV, 4 * A_HEADS,
             Q_LORA, KV_LORA, B_ROPE, 2 * D_MODEL)
    pts = np.cumsum((0,) + sizes)
    w_q, w_k, w_v, w_o, w_gate, w_cq, w_ckv, w_kr, w_gbr = (w_in[:, pts[i]:pts[i + 1]] for i in range(9))

    def pad_heads(w):
        w = w.reshape(D_MODEL, A_HEADS, A_DQK)
        return jnp.pad(w, ((0, 0), (0, 0), (0, LANE - A_DQK))).reshape(D_MODEL, A_HEADS * LANE)

    out = {}
    out["wa"] = jnp.concatenate([pad_heads(w_q) * (A_DQK ** -0.5), w_v, w_o], axis=1).astype(BF16)
    out["wc"] = jnp.concatenate([w_cq, w_ckv], axis=1).astype(BF16)
    wg4 = w_gate.reshape(D_MODEL, 4, A_HEADS)
    w_fal = jnp.stack([wg4[:, 1], wg4[:, 1], wg4[:, 3], wg4[:, 3]], axis=1).reshape(D_MODEL, 4 * A_HEADS)
    out["wkg"] = jnp.concatenate([w_kr, w_gate, w_fal], axis=1).astype(BF16)
    out["wg"] = w_gbr.astype(BF16)
    out["wkt"] = pad_heads(w_k).T.astype(BF16)
    out["wgt"] = w_gate.T.astype(BF16)
    bias = p["b_mgate"].astype(F32).reshape(4 * A_HEADS)
    b4 = bias.reshape(4, A_HEADS)
    b_fal = jnp.stack([b4[1], b4[1], b4[3], b4[3]]).reshape(4 * A_HEADS)
    out["bkg"] = jnp.concatenate([jnp.zeros((B_ROPE,), F32), bias, b_fal]).reshape(1, LANE)
    out["bgt"] = bias.reshape(4 * A_HEADS, 1)
    out["g_mix"] = p["norm_mix_g"].astype(F32).reshape(1, D_MODEL)
    out["g_mlstm"] = p["mlstm_norm_g"].astype(F32).reshape(1, A_HEADS * A_DV)
    out["gq"] = p["q_a_norm_g"].astype(F32).reshape(1, Q_LORA)
    out["gkv"] = p["kv_a_norm_g"].astype(F32).reshape(1, KV_LORA)
    wq = p["w_q_up"].reshape(Q_LORA, B_HEADS, B_DQK).transpose(1, 0, 2)
    out["wq"] = jnp.pad(wq, ((0, 0), (0, 0), (0, QK_PAD - B_DQK))).astype(BF16)
    wkv = p["w_kv_up"].reshape(KV_LORA, B_HEADS, B_NOPE + B_DV).transpose(1, 0, 2)
    out["wkbt"] = wkv[:, :, :B_NOPE].transpose(0, 2, 1).astype(BF16)
    out["wv"] = wkv[:, :, B_NOPE:].astype(BF16)
    out["nq"] = jnp.pad(p["qk_norm_q_g"].astype(F32), (0, QK_PAD - B_DQK)).reshape(1, QK_PAD)
    nk = p["qk_norm_k_g"].astype(F32)
    out["nk_col"] = jnp.pad(nk, (0, QK_PAD - B_DQK)).reshape(QK_PAD, 1)
    out["kb"] = (1.01 * B_DQK ** 0.5 * jnp.max(jnp.abs(nk))).reshape(1, 1)
    out["score_bound"] = 1.01 * B_DQK ** 0.5 * LOG2E * jnp.max(jnp.abs(nk)) * jnp.max(jnp.abs(p["qk_norm_q_g"]))
    out["wpa"] = p["w_proj_a"].astype(BF16)
    out["wpb"] = p["w_proj_b"].astype(BF16)
    out["wo"] = p["w_out"].astype(BF16)
    out["g_ffn"] = p["norm_ffn_g"].astype(F32).reshape(1, D_MODEL)
    out["wpq"] = p["w_peer_q"].astype(BF16)
    out["k1"] = p["peer_keys1"].astype(BF16)
    out["k2"] = p["peer_keys2"].astype(BF16)
    out["u"] = p["peer_u"].astype(BF16)
    out["vt"] = p["peer_v"].T.astype(BF16)
    return out


def _rope_tables(S):
    pos = jnp.arange(S, dtype=F32)
    inv = ROPE_THETA ** (-jnp.arange(0, B_ROPE, 2, dtype=F32) / B_ROPE)
    ang = pos[:, None] * inv[None, :]
    ang = jnp.concatenate([ang, ang], axis=-1)
    sign = jnp.where(jnp.arange(B_ROPE) < B_ROPE // 2, -1.0, 1.0).astype(F32)
    pad = ((0, 0), (0, LANE - B_ROPE))
    cos, sin = jnp.pad(jnp.cos(ang), pad), jnp.pad(jnp.sin(ang) * sign, pad)
    return cos, sin, cos.T, sin.T


def _pick(n, pref):
    t = min(n, pref)
    assert n % t == 0, (n, t)
    return t


def _layer(x, w):
    B, S, _ = x.shape
    T = B * S
    x2 = x.reshape(T, D_MODEL)
    qa, va, osig, cqkv, kg, gs, kt, gt = _inproj(
        x2, w["g_mix"], w["wa"], w["wc"], w["wkg"], w["wg"], w["wkt"], w["wgt"], w["bkg"], w["bgt"],
        _pick(T, 256))
    hf = _mlstm(qa, kt, va, kg, gt, B, S, False)
    ha = _mlstm(qa, kt, va, kg, gt, B, S, True, hf, osig, w["g_mlstm"])
    cos, sin, cost, sint = _rope_tables(S)
    ts = _pick(S, 256)
    nkt = jnp.broadcast_to(w["nk_col"], (QK_PAD, ts))
    q, kbt, v = _mlaprep(cqkv, kg, cos, sin, cost, sint, w["gq"], w["gkv"], w["wq"], w["wkbt"], w["wv"], w["nq"],
                         nkt, w["kb"], B, S, ts)
    tq, tk = _pick(S, 512), _pick(S, 512)
    ao = lax.cond(w["score_bound"] <= 50.0,
                  lambda *a: _attention(*a, tq, tk, True),
                  lambda *a: _attention(*a, tq, tk, False),
                  q, kbt, v).reshape(T, B_HEADS * B_DV)
    x1 = _merge(ha, ao, gs, x2, w["wpa"], w["wpb"], w["wo"], _pick(T, 512))
    xt, p1, p2, th = _route(x1, w["g_ffn"], w["wpq"], w["k1"], w["k2"], _pick(T, 256))
    y = _peer(xt, w["u"], w["vt"], p1, p2, th, x1, _pick(T, 256))
    return y.reshape(B, S, D_MODEL)


def kernel(x_prompt, x_sample, norm_mix_g, w_in, b_mgate, mlstm_norm_g, q_a_norm_g, w_q_up, kv_a_norm_g, w_kv_up, qk_norm_q_g, qk_norm_k_g, w_proj_a, w_proj_b, w_out, norm_ffn_g, w_peer_q, peer_keys1, peer_keys2, peer_u, peer_v):
    params = dict(norm_mix_g=norm_mix_g, w_in=w_in, b_mgate=b_mgate, mlstm_norm_g=mlstm_norm_g,
                  q_a_norm_g=q_a_norm_g, w_q_up=w_q_up, kv_a_norm_g=kv_a_norm_g, w_kv_up=w_kv_up,
                  qk_norm_q_g=qk_norm_q_g, qk_norm_k_g=qk_norm_k_g, w_proj_a=w_proj_a, w_proj_b=w_proj_b,
                  w_out=w_out, norm_ffn_g=norm_ffn_g, w_peer_q=w_peer_q, peer_keys1=peer_keys1,
                  peer_keys2=peer_keys2, peer_u=peer_u, peer_v=peer_v)
    depth = w_in.shape[0]
    layers = [_prep_weights({k: v[l] for k, v in params.items()}) for l in range(depth)]

    def run(x):
        for w in layers:
            x = _layer(x, w)
        return x

    return (run(x_prompt), run(x_sample))
```

```python
import functools

import jax
import jax.numpy as jnp
import numpy as np
from jax import lax
from jax.experimental import pallas as pl
from jax.experimental.pallas import tpu as pltpu

F32 = jnp.float32
BF16 = jnp.bfloat16

D_MODEL = 1024
A_HEADS, A_DQK, A_DV, A_CHUNK = 8, 64, 128, 128
B_HEADS, B_NOPE, B_ROPE, B_DV = 8, 128, 64, 128
B_DQK = B_NOPE + B_ROPE
Q_LORA, KV_LORA = 384, 256
ROPE_THETA = 10000.0
PEER_HEADS, PEER_NKEYS, PEER_DQ, PEER_TOPK = 8, 128, 256, 16
PEER_HALF = PEER_DQ // 2
PEER_EXPERTS = PEER_NKEYS * PEER_NKEYS
EPS = 1e-6

LANE = 128
QK_PAD = 2 * LANE
VMEM_LIMIT = 52 * 1024 * 1024


def _cparams(*sem):
    return pltpu.CompilerParams(dimension_semantics=sem, vmem_limit_bytes=VMEM_LIMIT)


def _dot(a, b):
    return jnp.dot(a, b, preferred_element_type=F32)


def _dot_nt(a, b):
    return lax.dot_general(a, b, (((1,), (1,)), ((), ())), preferred_element_type=F32)


def _rms(x, g):
    return x * lax.rsqrt(jnp.mean(x * x, axis=-1, keepdims=True) + EPS) * g


def _log_sigmoid(x):
    return jnp.minimum(x, 0.0) - jnp.log(1.0 + jnp.exp(-jnp.abs(x)))


def _sigmoid(x):
    return 1.0 / (1.0 + jnp.exp(-x))


def _const_spec(shape):
    nd = len(shape)
    return pl.BlockSpec(shape, lambda *_: (0,) * nd)


def _inproj_kernel(x_ref, g_ref, wa_ref, wc_ref, wkg_ref, wg_ref, wkt_ref, wgt_ref, bkg_ref, bgt_ref,
                   qa_ref, va_ref, os_ref, cqkv_ref, kg_ref, gs_ref, kt_ref, gt_ref):
    h = _rms(x_ref[...], g_ref[...]).astype(BF16)
    a = _dot(h, wa_ref[...])
    qa_ref[...] = a[:, :1024].astype(BF16)
    va_ref[...] = a[:, 1024:2048].astype(BF16)
    os_ref[...] = _sigmoid(a[:, 2048:]).astype(BF16)
    cqkv_ref[...] = _dot(h, wc_ref[...])
    kg = _dot(h, wkg_ref[...]) + bkg_ref[...]
    lane = lax.broadcasted_iota(jnp.int32, kg.shape, 1)
    is_f = ((lane >= 72) & (lane < 80)) | (lane >= 88)
    kg_ref[...] = jnp.where(is_f, _log_sigmoid(kg), kg)
    gs_ref[...] = _sigmoid(_dot(h, wg_ref[...])).astype(BF16)
    kt_ref[...] = _dot_nt(wkt_ref[...], h).astype(BF16)
    gt = _dot_nt(wgt_ref[...], h) + bgt_ref[...]
    row = lax.broadcasted_iota(jnp.int32, gt.shape, 0)
    is_fr = ((row >= 8) & (row < 16)) | (row >= 24)
    gt_ref[...] = jnp.where(is_fr, _log_sigmoid(gt), gt)


def _inproj(x2, g, wa, wc, wkg, wg, wkt, wgt, bkg, bgt, tm):
    T = x2.shape[0]
    row = lambda w: pl.BlockSpec((tm, w), lambda i: (i, 0))
    col = lambda r: pl.BlockSpec((r, tm), lambda i: (0, i))
    return pl.pallas_call(
        _inproj_kernel,
        grid=(T // tm,),
        in_specs=[row(D_MODEL), _const_spec(g.shape), _const_spec(wa.shape), _const_spec(wc.shape),
                  _const_spec(wkg.shape), _const_spec(wg.shape), _const_spec(wkt.shape),
                  _const_spec(wgt.shape), _const_spec(bkg.shape), _const_spec(bgt.shape)],
        out_specs=[row(1024), row(1024), row(1024), row(640), row(128), row(2048), col(1024), col(32)],
        out_shape=[jax.ShapeDtypeStruct((T, 1024), BF16), jax.ShapeDtypeStruct((T, 1024), BF16),
                   jax.ShapeDtypeStruct((T, 1024), BF16), jax.ShapeDtypeStruct((T, 640), F32),
                   jax.ShapeDtypeStruct((T, 128), F32), jax.ShapeDtypeStruct((T, 2048), BF16),
                   jax.ShapeDtypeStruct((1024, T), BF16), jax.ShapeDtypeStruct((32, T), F32)],
        compiler_params=_cparams("parallel"),
        name="inproj",
    )(x2, g, wa, wc, wkg, wg, wkt, wgt, bkg, bgt)


def _split3(x):
    hi = x.astype(BF16)
    r = x - hi.astype(F32)
    mid = r.astype(BF16)
    lo = (r - mid.astype(F32)).astype(BF16)
    return hi, mid, lo


def _dot3(a_f32, b_bf16):
    hi, mid, lo = _split3(a_f32)
    return _dot(hi, b_bf16) + _dot(mid, b_bf16) + _dot(lo, b_bf16)


def _dot3r(a_bf16, b_f32):
    hi, mid, lo = _split3(b_f32)
    return _dot(a_bf16, hi) + _dot(a_bf16, mid) + _dot(a_bf16, lo)


def _cummax_rows(y, reverse):
    n = y.shape[0]
    row = lax.broadcasted_iota(jnp.int32, y.shape, 0)
    k = 1
    while k < n:
        if reverse:
            sh = jnp.where(row < n - k, pltpu.roll(y, n - k, 0), -jnp.inf)
        else:
            sh = jnp.where(row >= k, pltpu.roll(y, k, 0), -jnp.inf)
        y = jnp.maximum(y, sh)
        k *= 2
    return y


def _mlstm_kernel(*refs, reverse):
    if reverse:
        q_ref, kt_ref, v_ref, kg_ref, gt_ref, hf_ref, os_ref, ng_ref, out_ref, cn_scr, m_scr = refs
    else:
        q_ref, kt_ref, v_ref, kg_ref, gt_ref, out_ref, cn_scr, m_scr = refs
    L = A_CHUNK

    @pl.when(pl.program_id(1) == 0)
    def _():
        cn_scr[...] = jnp.zeros_like(cn_scr)
        m_scr[...] = jnp.zeros_like(m_scr)

    row = lax.broadcasted_iota(jnp.int32, (L, L), 0)
    col = lax.broadcasted_iota(jnp.int32, (L, L), 1)
    mask = (col >= row) if reverse else (col <= row)
    tri_c = jnp.where(mask, 1.0, 0.0).astype(BF16)
    tri_r = jnp.where((row >= col) if reverse else (row <= col), 1.0, 0.0).astype(BF16)

    g = kg_ref[:, 64:96]
    gal = kg_ref[:, 96:128]
    gt = gt_ref[...]
    b_al = _dot3r(tri_c, gal)
    b_rows = _dot3(gt, tri_r)
    y = jnp.concatenate([g - b_al, jnp.zeros((L, LANE - 4 * A_HEADS), F32)], axis=1)
    cm = _cummax_rows(y, reverse)[:, :4 * A_HEADS]
    bc3 = jnp.concatenate(_split3(jnp.concatenate([b_al, cm], axis=1)), axis=1)
    off = 16 if reverse else 0
    ones = jnp.ones((L, LANE), BF16)
    sel_r = lax.broadcasted_iota(jnp.int32, (6 * 4 * A_HEADS, 2 * LANE), 0)
    sel_c = lax.broadcasted_iota(jnp.int32, (6 * 4 * A_HEADS, 2 * LANE), 1)
    sel_r = sel_r % (8 * A_HEADS) - jnp.where(sel_c >= LANE, 4 * A_HEADS, 0)
    last = 0 if reverse else L - 1

    H = range(A_HEADS)
    sls = [slice(h * LANE, (h + 1) * LANE) for h in H]
    bc = [_dot(bc3, jnp.where(sel_r == off + h, 1.0, 0.0).astype(BF16)) for h in H]
    b_c = [x[:, :LANE] for x in bc]
    cm_c = [x[:, LANE:] for x in bc]
    qk = [_dot(q_ref[:, sls[h]], kt_ref[sls[h], :]) for h in H]
    cns = [cn_scr[h] for h in H]
    qc = [_dot(q_ref[:, sls[h]], cns[h].astype(BF16)) for h in H]
    ms = [m_scr[h] for h in H]
    i_r = [gt[off + h:off + h + 1, :] for h in H]
    b_r = [b_rows[off + 8 + h:off + 9 + h, :] for h in H]
    m_t = [b_c[h] + jnp.maximum(cm_c[h], ms[h]) for h in H]
    s = [(qk[h] * jnp.where(mask, jnp.exp(b_c[h] - b_r[h] + i_r[h] - m_t[h]), 0.0)).astype(BF16) for h in H]
    vext = [jnp.concatenate([v_ref[:, sls[h]], ones], axis=1) for h in H]
    sv = [_dot(s[h], vext[h]) for h in H]
    b_last = [b_c[h][last:last + 1] for h in H]
    m_new = [b_last[h] + jnp.maximum(ms[h], cm_c[h][last:last + 1]) for h in H]
    kts = [(kt_ref[sls[h], :].astype(F32) * jnp.exp(b_last[h] - b_r[h] + i_r[h] - m_new[h])).astype(BF16) for h in H]
    kv = [_dot(kts[h], vext[h]) for h in H]
    for h in H:
        decay = jnp.exp(b_last[h] + ms[h] - m_new[h])
        cn_scr[h] = jnp.concatenate([decay, decay], axis=1) * cns[h] + kv[h]
        m_scr[h] = m_new[h]
        inter_w = jnp.exp(b_c[h] + ms[h] - m_t[h])
        tot = sv[h] + jnp.concatenate([inter_w, inter_w], axis=1) * qc[h]
        hh = tot[:, :LANE] / jnp.maximum(jnp.abs(tot[:, LANE:]), jnp.exp(-m_t[h]))
        if reverse:
            t = hh + hf_ref[:, sls[h]]
            y = t * lax.rsqrt(jnp.mean(t * t, axis=-1, keepdims=True) + EPS) * ng_ref[:, sls[h]]
            out_ref[:, sls[h]] = (y * os_ref[:, sls[h]].astype(F32)).astype(BF16)
        else:
            out_ref[:, sls[h]] = hh


def _mlstm(qa, kt, va, kg, gt, B, S, reverse, hf=None, osig=None, ng=None):
    L = A_CHUNK
    nc = S // L
    T = B * S
    if reverse:
        blk = lambda b, c: b * nc + (nc - 1 - c)
    else:
        blk = lambda b, c: b * nc + c
    row = lambda w: pl.BlockSpec((L, w), lambda b, c: (blk(b, c), 0))
    col = lambda r: pl.BlockSpec((r, L), lambda b, c: (0, blk(b, c)))
    in_specs = [row(1024), col(1024), row(1024), row(128), col(32)]
    args = [qa, kt, va, kg, gt]
    if reverse:
        in_specs += [row(1024), row(1024), _const_spec(ng.shape)]
        args += [hf, osig, ng]
    return pl.pallas_call(
        functools.partial(_mlstm_kernel, reverse=reverse),
        grid=(B, nc),
        in_specs=in_specs,
        out_specs=row(1024),
        out_shape=jax.ShapeDtypeStruct((T, 1024), BF16 if reverse else F32),
        scratch_shapes=[pltpu.VMEM((A_HEADS, LANE, 2 * LANE), F32), pltpu.VMEM((A_HEADS, 1, LANE), F32)],
        compiler_params=_cparams("parallel", "arbitrary"),
        name="mlstm_bwd" if reverse else "mlstm_fwd",
    )(*args)


def _rope(x, cos, sin_signed, axis):
    idx = lax.broadcasted_iota(jnp.int32, x.shape, axis)
    half = B_ROPE // 2
    rot = jnp.where(idx < half, pltpu.roll(x, LANE - half, axis), pltpu.roll(x, half, axis))
    return x * cos + rot * sin_signed


BOUND_COL = B_DQK - LANE
LOG2E = 1.4426950408889634


def _mlaprep_kernel(cqkv_ref, kg_ref, cos_ref, sin_ref, cost_ref, sint_ref, gq_ref, gkv_ref, wq_ref, wkt_ref,
                    wv_ref, nq_ref, nkt_ref, kb_ref, q_ref, kt_ref, v_ref):
    ts = cqkv_ref.shape[0]
    cq = _rms(cqkv_ref[:, :Q_LORA], gq_ref[...]).astype(BF16)
    ckv = _rms(cqkv_ref[:, Q_LORA:], gkv_ref[...]).astype(BF16)
    kgt = kg_ref[...].T
    rowi = lax.broadcasted_iota(jnp.int32, kgt.shape, 0)
    krt = jnp.where(rowi < B_ROPE, kgt, 0.0)
    ss_r = jnp.sum(krt * krt, axis=0, keepdims=True)
    nkt = nkt_ref[...]
    krt_rot = _rope(krt * nkt[LANE:], cost_ref[...], sint_ref[...], 0)
    one_row = jnp.where(rowi == BOUND_COL, 1.0, 0.0)
    nq = nq_ref[...]
    cos, sin = cos_ref[...], sin_ref[...]
    lane = lax.broadcasted_iota(jnp.int32, (ts, LANE), 1)
    scale = B_DQK ** -0.5 * LOG2E
    kb = kb_ref[...]
    ones = jnp.ones((ts, B_DV), BF16)
    for h in range(B_HEADS):
        q = _dot(cq, wq_ref[h])
        rq = lax.rsqrt(jnp.sum(q * q, axis=-1, keepdims=True) * (1.0 / B_DQK) + EPS) * scale
        qn = q * nq
        qnorm = jnp.sqrt(jnp.sum(qn * qn, axis=-1, keepdims=True)) * rq
        q_ref[0, h, :, :LANE] = (qn[:, :LANE] * rq).astype(BF16)
        q_rot = _rope(qn[:, LANE:], cos, sin, 1) * rq
        q_ref[0, h, :, LANE:] = jnp.where(lane == BOUND_COL, -(qnorm * kb), q_rot).astype(BF16)
        knt = _dot_nt(wkt_ref[h], ckv)
        rk = lax.rsqrt((jnp.sum(knt * knt, axis=0, keepdims=True) + ss_r) * (1.0 / B_DQK) + EPS)
        kt_ref[0, h, :LANE, :] = (knt * nkt[:LANE] * rk).astype(BF16)
        kt_ref[0, h, LANE:, :] = (krt_rot * rk + one_row).astype(BF16)
        v_ref[0, h, :, :B_DV] = _dot(ckv, wv_ref[h]).astype(BF16)
        v_ref[0, h, :, B_DV:] = ones


def _mlaprep(cqkv, kg, cos, sin, cost, sint, gq, gkv, wq, wkt, wv, nq, nkt, kb, B, S, ts):
    ns = S // ts
    row = lambda w: pl.BlockSpec((ts, w), lambda b, i: (b * ns + i, 0))
    pos = pl.BlockSpec((ts, LANE), lambda b, i: (i, 0))
    post = pl.BlockSpec((LANE, ts), lambda b, i: (0, i))
    consts = (gq, gkv, wq, wkt, wv, nq, nkt, kb)
    return pl.pallas_call(
        _mlaprep_kernel,
        grid=(B, ns),
        in_specs=[row(640), row(128), pos, pos, post, post] + [_const_spec(a.shape) for a in consts],
        out_specs=[pl.BlockSpec((1, B_HEADS, ts, QK_PAD), lambda b, i: (b, 0, i, 0)),
                   pl.BlockSpec((1, B_HEADS, QK_PAD, ts), lambda b, i: (b, 0, 0, i)),
                   pl.BlockSpec((1, B_HEADS, ts, 2 * B_DV), lambda b, i: (b, 0, i, 0))],
        out_shape=[jax.ShapeDtypeStruct((B, B_HEADS, S, QK_PAD), BF16),
                   jax.ShapeDtypeStruct((B, B_HEADS, QK_PAD, S), BF16),
                   jax.ShapeDtypeStruct((B, B_HEADS, S, 2 * B_DV), BF16)],
        compiler_params=_cparams("parallel", "parallel"),
        name="mlaprep",
    )(cqkv, kg, cos, sin, cost, sint, *consts)


def _attn_kernel(q_ref, kt_ref, v_ref, o_ref, *, tk, nk):
    q = q_ref[0, 0]
    tq = q.shape[0]

    def body(j, carry):
        m, acc = carry
        start = pl.multiple_of(j * tk, tk)
        kt = kt_ref[0, 0, :, pl.ds(start, tk)]
        v = v_ref[0, 0, pl.ds(start, tk), :]
        s = _dot(q, kt)
        m_new = jnp.maximum(m, jnp.max(s, axis=1, keepdims=True))
        p = jnp.exp2(s - m_new)
        acc = jnp.exp2(m - m_new) * acc + _dot(p.astype(BF16), v)
        return m_new, acc

    init = (jnp.full((tq, 1), -jnp.inf, F32), jnp.zeros((tq, 2 * B_DV), F32))
    _, acc = lax.fori_loop(0, nk, body, init)
    o_ref[0] = (acc[:, :B_DV] / acc[:, B_DV:]).astype(BF16)


def _attn_fast_kernel(q_ref, kt_ref, v_ref, o_ref, *, tk, nk, unroll):
    q = q_ref[0, 0]
    tq = q.shape[0]

    def probs(j):
        kt = kt_ref[0, 0, :, pl.ds(pl.multiple_of(j * tk, tk), tk)]
        return jnp.exp2(_dot(q, kt)).astype(BF16)

    def weighted(j, p, acc):
        v = v_ref[0, 0, pl.ds(pl.multiple_of(j * tk, tk), tk), :]
        return acc + _dot(p, v)

    def body(j, carry):
        p, acc = carry
        return probs(j), weighted(j - 1, p, acc)

    p, acc = lax.fori_loop(1, nk, body, (probs(0), jnp.zeros((tq, 2 * B_DV), F32)), unroll=unroll)
    acc = weighted(nk - 1, p, acc)
    o_ref[0] = (acc[:, :B_DV] / acc[:, B_DV:]).astype(BF16)


def _attention(q, kt, v, tq, tk, fast):
    B, H, S, _ = q.shape
    nk = S // tk
    if fast:
        body, name = functools.partial(_attn_fast_kernel, tk=tk, nk=nk, unroll=max(1, min(8, nk - 1))), "attention_fast"
    else:
        body, name = functools.partial(_attn_kernel, tk=tk, nk=nk), "attention"
    return pl.pallas_call(
        body,
        grid=(B, H, S // tq),
        in_specs=[pl.BlockSpec((1, 1, tq, QK_PAD), lambda b, h, i: (b, h, i, 0)),
                  pl.BlockSpec((1, 1, QK_PAD, S), lambda b, h, i: (b, h, 0, 0)),
                  pl.BlockSpec((1, 1, S, 2 * B_DV), lambda b, h, i: (b, h, 0, 0))],
        out_specs=pl.BlockSpec((1, tq, B_DV), lambda b, h, i: (b, i, h)),
        out_shape=jax.ShapeDtypeStruct((B, S, H * B_DV), BF16),
        compiler_params=_cparams("parallel", "parallel", "arbitrary"),
        name=name,
    )(q, kt, v)


def _merge_kernel(ha_ref, ao_ref, gs_ref, x_ref, wpa_ref, wpb_ref, wo_ref, o_ref):
    ya = _dot(ha_ref[...], wpa_ref[...])
    yb = _dot(ao_ref[...], wpb_ref[...])
    merged = gs_ref[:, :D_MODEL].astype(F32) * ya + gs_ref[:, D_MODEL:].astype(F32) * yb
    o_ref[...] = x_ref[...] + _dot(merged.astype(BF16), wo_ref[...])


def _merge(ha, ao, gs, x2, wpa, wpb, wo, tm):
    T = x2.shape[0]
    row = lambda w: pl.BlockSpec((tm, w), lambda i: (i, 0))
    return pl.pallas_call(
        _merge_kernel,
        grid=(T // tm,),
        in_specs=[row(1024), row(1024), row(2048), row(1024)] + [_const_spec(w.shape) for w in (wpa, wpb, wo)],
        out_specs=row(1024),
        out_shape=jax.ShapeDtypeStruct((T, D_MODEL), F32),
        compiler_params=_cparams("parallel"),
        name="merge",
    )(ha, ao, gs, x2, wpa, wpb, wo)


N_EXTRACT = PEER_TOPK + 1
ROUTE_HEADS_PER_ITER = 4


def _top_rows(s, n):
    rows = []
    for _ in range(n):
        m = jnp.max(s, axis=0, keepdims=True)
        rows.append(m)
        s = jnp.where(s == m, -jnp.inf, s)
    return rows


def _stack8(rows, t):
    ri = lax.broadcasted_iota(jnp.int32, (8, t), 0)
    out = jnp.full((8, t), -jnp.inf, F32)
    for k, r in enumerate(rows):
        out = jnp.where(ri == k, r, out)
    return out


def _route_head(h, q_scr, k1, k2, p1_ref, p2_ref, th_ref):
    qh = q_scr[h]
    t = qh.shape[0]
    s1 = _dot_nt(k1, qh[:, :PEER_HALF])
    s2 = _dot_nt(k2, qh[:, PEER_HALF:])
    v1 = _top_rows(s1, N_EXTRACT)
    v2 = _top_rows(s2, N_EXTRACT)
    v2a, v2b, v2c = _stack8(v2[0:8], t), _stack8(v2[8:16], t), _stack8(v2[16:17], t)
    v1b, v1c = _stack8(v1[8:16], t), _stack8(v1[16:17], t)
    cand = jnp.concatenate([v1[i] + v2a for i in range(8)] +
                           [v1[0] + v2b, v1[0] + v2c, v1b + v2[0], v1c + v2[0]], axis=0)
    best = _top_rows(cand, N_EXTRACT)
    z = jnp.zeros_like(best[0])
    for b in best[:PEER_TOPK]:
        z = z + jnp.exp(b - best[0])
    theta = 0.5 * (best[PEER_TOPK - 1] + best[PEER_TOPK])
    m1, m2 = v1[0], v2[0]
    c = RSQRT2 / z
    p1_ref[h] = jnp.exp(s1 - m1) * c
    p2_ref[h] = jnp.exp(s2 - m2)
    th_ref[pl.ds(h, 1), :] = jnp.exp(theta - m1 - m2) * c


def _route_kernel(x_ref, g_ref, wq_ref, k1_ref, k2_ref, xt_ref, p1_ref, p2_ref, th_ref, q_scr):
    xn = _rms(x_ref[...], g_ref[...])
    xt_ref[...] = xn.T.astype(BF16)
    q = _dot(xn.astype(BF16), wq_ref[...])
    for h in range(PEER_HEADS):
        q_scr[h] = q[:, h * PEER_DQ:(h + 1) * PEER_DQ].astype(BF16)
    k1, k2 = k1_ref[...], k2_ref[...]

    def head_group(i, carry):
        for k in range(ROUTE_HEADS_PER_ITER):
            _route_head(ROUTE_HEADS_PER_ITER * i + k, q_scr, k1, k2, p1_ref, p2_ref, th_ref)
        return carry

    lax.fori_loop(0, PEER_HEADS // ROUTE_HEADS_PER_ITER, head_group, 0)


def _route(x1, g, wq, k1, k2, tr):
    T = x1.shape[0]
    return pl.pallas_call(
        _route_kernel,
        grid=(T // tr,),
        in_specs=[pl.BlockSpec((tr, D_MODEL), lambda i: (i, 0)), _const_spec(g.shape), _const_spec(wq.shape),
                  _const_spec(k1.shape), _const_spec(k2.shape)],
        out_specs=[pl.BlockSpec((D_MODEL, tr), lambda i: (0, i)),
                   pl.BlockSpec((PEER_HEADS, PEER_NKEYS, tr), lambda i: (0, 0, i)),
                   pl.BlockSpec((PEER_HEADS, PEER_NKEYS, tr), lambda i: (0, 0, i)),
                   pl.BlockSpec((PEER_HEADS, tr), lambda i: (0, i))],
        out_shape=[jax.ShapeDtypeStruct((D_MODEL, T), BF16),
                   jax.ShapeDtypeStruct((PEER_HEADS, PEER_NKEYS, T), F32),
                   jax.ShapeDtypeStruct((PEER_HEADS, PEER_NKEYS, T), F32),
                   jax.ShapeDtypeStruct((PEER_HEADS, T), F32)],
        scratch_shapes=[pltpu.VMEM((PEER_HEADS, tr, PEER_DQ), BF16)],
        compiler_params=_cparams("parallel"),
        name="route",
    )(x1, g, wq, k1, k2)


E1_PER_STEP = 32
RSQRT2 = 0.7071067811865476


def _peer_kernel(xt_ref, u_ref, vt_ref, p1_ref, p2_ref, th_ref, x1_ref, o_ref, acc_scr, g_scr):
    s = pl.program_id(0)
    nblk = PEER_EXPERTS // (E1_PER_STEP * PEER_NKEYS)

    @pl.when(s == 0)
    def _():
        acc_scr[...] = jnp.zeros_like(acc_scr)
        g_scr[...] = jnp.zeros_like(g_scr)

    @pl.when((lax.rem(s, nblk) == 1) & (s > 1))
    def _():
        o_ref[...] = x1_ref[...] + acc_scr[...].T
        acc_scr[...] = jnp.zeros_like(acc_scr)

    slot = lax.rem(s, 2)
    acc_scr[...] += _dot(vt_ref[...], g_scr[1 - slot])
    xt = xt_ref[...]
    for e in range(E1_PER_STEP):
        sl = slice(e * PEER_NKEYS, (e + 1) * PEER_NKEYS)
        a = _dot(u_ref[sl, :], xt)
        act = a * (1.0 + lax.erf(a))
        w = jnp.zeros_like(a)
        for h in range(PEER_HEADS):
            p = p2_ref[h] * p1_ref[h, e:e + 1, :]
            w = w + jnp.where(p >= th_ref[h:h + 1, :], p, 0.0)
        g_scr[slot, sl, :] = (w * act).astype(BF16)


def _peer(xt, u, vt, p1, p2, th, x1, tb):
    T = x1.shape[0]
    eb = E1_PER_STEP * PEER_NKEYS
    nblk = PEER_EXPERTS // eb
    nt = T // tb
    cur = lambda s: jnp.minimum(s // nblk, nt - 1)
    done = lambda s: jnp.maximum(s - 2, 0) // nblk
    return pl.pallas_call(
        _peer_kernel,
        grid=(nt * nblk + 2,),
        in_specs=[pl.BlockSpec((D_MODEL, tb), lambda s: (0, cur(s))),
                  pl.BlockSpec((eb, D_MODEL), lambda s: (s % nblk, 0)),
                  pl.BlockSpec((D_MODEL, eb), lambda s: (0, (s + nblk - 1) % nblk)),
                  pl.BlockSpec((PEER_HEADS, E1_PER_STEP, tb), lambda s: (0, s % nblk, cur(s))),
                  pl.BlockSpec((PEER_HEADS, PEER_NKEYS, tb), lambda s: (0, 0, cur(s))),
                  pl.BlockSpec((PEER_HEADS, tb), lambda s: (0, cur(s))),
                  pl.BlockSpec((tb, D_MODEL), lambda s: (done(s), 0))],
        out_specs=pl.BlockSpec((tb, D_MODEL), lambda s: (done(s), 0)),
        out_shape=jax.ShapeDtypeStruct((T, D_MODEL), F32),
        scratch_shapes=[pltpu.VMEM((D_MODEL, tb), F32), pltpu.VMEM((2, eb, tb), BF16)],
        compiler_params=_cparams("arbitrary"),
        name="peer",
    )(xt, u, vt, p1, p2, th, x1)


def _prep_weights(p):
    w_in = p["w_in"]
    sizes = (A_HEADS * A_DQK, A_HEADS * A_DQK, A_HEADS * A_DV, A_HEADS * A_DV, 4 * A_HEADS,
             Q_LORA, KV_LORA, B_ROPE, 2 * D_MODEL)
    pts = np.cumsum((0,) + sizes)
    w_q, w_k, w_v, w_o, w_gate, w_cq, w_ckv, w_kr, w_gbr = (w_in[:, pts[i]:pts[i + 1]] for i in range(9))

    def pad_heads(w):
        w = w.reshape(D_MODEL, A_HEADS, A_DQK)
        return jnp.pad(w, ((0, 0), (0, 0), (0, LANE - A_DQK))).reshape(D_MODEL, A_HEADS * LANE)

    out = {}
    out["wa"] = jnp.concatenate([pad_heads(w_q) * (A_DQK ** -0.5), w_v, w_o], axis=1).astype(BF16)
    out["wc"] = jnp.concatenate([w_cq, w_ckv], axis=1).astype(BF16)
    wg4 = w_gate.reshape(D_MODEL, 4, A_HEADS)
    w_fal = jnp.stack([wg4[:, 1], wg4[:, 1], wg4[:, 3], wg4[:, 3]], axis=1).reshape(D_MODEL, 4 * A_HEADS)
    out["wkg"] = jnp.concatenate([w_kr, w_gate, w_fal], axis=1).astype(BF16)
    out["wg"] = w_gbr.astype(BF16)
    out["wkt"] = pad_heads(w_k).T.astype(BF16)
    out["wgt"] = w_gate.T.astype(BF16)
    bias = p["b_mgate"].astype(F32).reshape(4 * A_HEADS)
    b4 = bias.reshape(4, A_HEADS)
    b_fal = jnp.stack([b4[1], b4[1], b4[3], b4[3]]).reshape(4 * A_HEADS)
    out["bkg"] = jnp.concatenate([jnp.zeros((B_ROPE,), F32), bias, b_fal]).reshape(1, LANE)
    out["bgt"] = bias.reshape(4 * A_HEADS, 1)
    out["g_mix"] = p["norm_mix_g"].astype(F32).reshape(1, D_MODEL)
    out["g_mlstm"] = p["mlstm_norm_g"].astype(F32).reshape(1, A_HEADS * A_DV)
    out["gq"] = p["q_a_norm_g"].astype(F32).reshape(1, Q_LORA)
    out["gkv"] = p["kv_a_norm_g"].astype(F32).reshape(1, KV_LORA)
    wq = p["w_q_up"].reshape(Q_LORA, B_HEADS, B_DQK).transpose(1, 0, 2)
    out["wq"] = jnp.pad(wq, ((0, 0), (0, 0), (0, QK_PAD - B_DQK))).astype(BF16)
    wkv = p["w_kv_up"].reshape(KV_LORA, B_HEADS, B_NOPE + B_DV).transpose(1, 0, 2)
    out["wkbt"] = wkv[:, :, :B_NOPE].transpose(0, 2, 1).astype(BF16)
    out["wv"] = wkv[:, :, B_NOPE:].astype(BF16)
    out["nq"] = jnp.pad(p["qk_norm_q_g"].astype(F32), (0, QK_PAD - B_DQK)).reshape(1, QK_PAD)
    nk = p["qk_norm_k_g"].astype(F32)
    out["nk_col"] = jnp.pad(nk, (0, QK_PAD - B_DQK)).reshape(QK_PAD, 1)
    out["kb"] = (1.01 * B_DQK ** 0.5 * jnp.max(jnp.abs(nk))).reshape(1, 1)
    out["score_bound"] = 1.01 * B_DQK ** 0.5 * LOG2E * jnp.max(jnp.abs(nk)) * jnp.max(jnp.abs(p["qk_norm_q_g"]))
    out["wpa"] = p["w_proj_a"].astype(BF16)
    out["wpb"] = p["w_proj_b"].astype(BF16)
    out["wo"] = p["w_out"].astype(BF16)
    out["g_ffn"] = p["norm_ffn_g"].astype(F32).reshape(1, D_MODEL)
    out["wpq"] = p["w_peer_q"].astype(BF16)
    out["k1"] = p["peer_keys1"].astype(BF16)
    out["k2"] = p["peer_keys2"].astype(BF16)
    out["u"] = (p["peer_u"] * RSQRT2).astype(BF16)
    out["vt"] = p["peer_v"].T.astype(BF16)
    return out


def _rope_tables(S):
    pos = jnp.arange(S, dtype=F32)
    inv = ROPE_THETA ** (-jnp.arange(0, B_ROPE, 2, dtype=F32) / B_ROPE)
    ang = pos[:, None] * inv[None, :]
    ang = jnp.concatenate([ang, ang], axis=-1)
    sign = jnp.where(jnp.arange(B_ROPE) < B_ROPE // 2, -1.0, 1.0).astype(F32)
    pad = ((0, 0), (0, LANE - B_ROPE))
    cos, sin = jnp.pad(jnp.cos(ang), pad), jnp.pad(jnp.sin(ang) * sign, pad)
    return cos, sin, cos.T, sin.T


def _pick(n, pref):
    t = min(n, pref)
    assert n % t == 0, (n, t)
    return t


def _layer(x, w):
    B, S, _ = x.shape
    T = B * S
    x2 = x.reshape(T, D_MODEL)
    qa, va, osig, cqkv, kg, gs, kt, gt = _inproj(
        x2, w["g_mix"], w["wa"], w["wc"], w["wkg"], w["wg"], w["wkt"], w["wgt"], w["bkg"], w["bgt"],
        _pick(T, 256))
    hf = _mlstm(qa, kt, va, kg, gt, B, S, False)
    ha = _mlstm(qa, kt, va, kg, gt, B, S, True, hf, osig, w["g_mlstm"])
    cos, sin, cost, sint = _rope_tables(S)
    ts = _pick(S, 256)
    nkt = jnp.broadcast_to(w["nk_col"], (QK_PAD, ts))
    q, kbt, v = _mlaprep(cqkv, kg, cos, sin, cost, sint, w["gq"], w["gkv"], w["wq"], w["wkbt"], w["wv"], w["nq"],
                         nkt, w["kb"], B, S, ts)
    tq, tk = _pick(S, 512), _pick(S, 512)
    ao = lax.cond(w["score_bound"] <= 50.0,
                  lambda *a: _attention(*a, tq, tk, True),
                  lambda *a: _attention(*a, tq, tk, False),
                  q, kbt, v).reshape(T, B_HEADS * B_DV)
    x1 = _merge(ha, ao, gs, x2, w["wpa"], w["wpb"], w["wo"], _pick(T, 512))
    xt, p1, p2, th = _route(x1, w["g_ffn"], w["wpq"], w["k1"], w["k2"], _pick(T, 256))
    y = _peer(xt, w["u"], w["vt"], p1, p2, th, x1, _pick(T, 256))
    return y.reshape(B, S, D_MODEL)


def kernel(x_prompt, x_sample, norm_mix_g, w_in, b_mgate, mlstm_norm_g, q_a_norm_g, w_q_up, kv_a_norm_g, w_kv_up, qk_norm_q_g, qk_norm_k_g, w_proj_a, w_proj_b, w_out, norm_ffn_g, w_peer_q, peer_keys1, peer_keys2, peer_u, peer_v):
    params = dict(norm_mix_g=norm_mix_g, w_in=w_in, b_mgate=b_mgate, mlstm_norm_g=mlstm_norm_g,
                  q_a_norm_g=q_a_norm_g, w_q_up=w_q_up, kv_a_norm_g=kv_a_norm_g, w_kv_up=w_kv_up,
                  qk_norm_q_g=qk_norm_q_g, qk_norm_k_g=qk_norm_k_g, w_proj_a=w_proj_a, w_proj_b=w_proj_b,
                  w_out=w_out, norm_ffn_g=norm_ffn_g, w_peer_q=w_peer_q, peer_keys1=peer_keys1,
                  peer_keys2=peer_keys2, peer_u=peer_u, peer_v=peer_v)
    depth = w_in.shape[0]
    layers = [_prep_weights({k: v[l] for k, v in params.items()}) for l in range(depth)]

    def run(x):
        for w in layers:
            x = _layer(x, w)
        return x

    return (run(x_prompt), run(x_sample))
```

```python
import functools

import jax
import jax.numpy as jnp
import numpy as np
from jax import lax
from jax.experimental import pallas as pl
from jax.experimental.pallas import tpu as pltpu

F32 = jnp.float32
BF16 = jnp.bfloat16

D_MODEL = 1024
A_HEADS, A_DQK, A_DV, A_CHUNK = 8, 64, 128, 128
B_HEADS, B_NOPE, B_ROPE, B_DV = 8, 128, 64, 128
B_DQK = B_NOPE + B_ROPE
Q_LORA, KV_LORA = 384, 256
ROPE_THETA = 10000.0
PEER_HEADS, PEER_NKEYS, PEER_DQ, PEER_TOPK = 8, 128, 256, 16
PEER_HALF = PEER_DQ // 2
PEER_EXPERTS = PEER_NKEYS * PEER_NKEYS
EPS = 1e-6

LANE = 128
QK_PAD = 2 * LANE
VMEM_LIMIT = 52 * 1024 * 1024


def _cparams(*sem):
    return pltpu.CompilerParams(dimension_semantics=sem, vmem_limit_bytes=VMEM_LIMIT)


def _dot(a, b):
    return jnp.dot(a, b, preferred_element_type=F32)


def _dot_nt(a, b):
    return lax.dot_general(a, b, (((1,), (1,)), ((), ())), preferred_element_type=F32)


def _rms(x, g):
    return x * lax.rsqrt(jnp.mean(x * x, axis=-1, keepdims=True) + EPS) * g


def _log_sigmoid(x):
    return jnp.minimum(x, 0.0) - jnp.log(1.0 + jnp.exp(-jnp.abs(x)))


def _sigmoid(x):
    return 1.0 / (1.0 + jnp.exp(-x))


def _const_spec(shape):
    nd = len(shape)
    return pl.BlockSpec(shape, lambda *_: (0,) * nd)


def _inproj_kernel(x_ref, g_ref, wa_ref, wc_ref, wkg_ref, wg_ref, wkt_ref, wgt_ref, bkg_ref, bgt_ref,
                   qa_ref, va_ref, os_ref, cqkv_ref, kg_ref, gs_ref, kt_ref, gt_ref):
    h = _rms(x_ref[...], g_ref[...]).astype(BF16)
    a = _dot(h, wa_ref[...])
    qa_ref[...] = a[:, :1024].astype(BF16)
    va_ref[...] = a[:, 1024:2048].astype(BF16)
    os_ref[...] = _sigmoid(a[:, 2048:]).astype(BF16)
    cqkv_ref[...] = _dot(h, wc_ref[...])
    kg = _dot(h, wkg_ref[...]) + bkg_ref[...]
    lane = lax.broadcasted_iota(jnp.int32, kg.shape, 1)
    is_f = ((lane >= 72) & (lane < 80)) | (lane >= 88)
    kg_ref[...] = jnp.where(is_f, _log_sigmoid(kg), kg)
    gs_ref[...] = _sigmoid(_dot(h, wg_ref[...])).astype(BF16)
    kt_ref[...] = _dot_nt(wkt_ref[...], h).astype(BF16)
    gt = _dot_nt(wgt_ref[...], h) + bgt_ref[...]
    row = lax.broadcasted_iota(jnp.int32, gt.shape, 0)
    is_fr = ((row >= 8) & (row < 16)) | (row >= 24)
    gt_ref[...] = jnp.where(is_fr, _log_sigmoid(gt), gt)


def _inproj(x2, g, wa, wc, wkg, wg, wkt, wgt, bkg, bgt, tm):
    T = x2.shape[0]
    row = lambda w: pl.BlockSpec((tm, w), lambda i: (i, 0))
    col = lambda r: pl.BlockSpec((r, tm), lambda i: (0, i))
    return pl.pallas_call(
        _inproj_kernel,
        grid=(T // tm,),
        in_specs=[row(D_MODEL), _const_spec(g.shape), _const_spec(wa.shape), _const_spec(wc.shape),
                  _const_spec(wkg.shape), _const_spec(wg.shape), _const_spec(wkt.shape),
                  _const_spec(wgt.shape), _const_spec(bkg.shape), _const_spec(bgt.shape)],
        out_specs=[row(1024), row(1024), row(1024), row(640), row(128), row(2048), col(1024), col(32)],
        out_shape=[jax.ShapeDtypeStruct((T, 1024), BF16), jax.ShapeDtypeStruct((T, 1024), BF16),
                   jax.ShapeDtypeStruct((T, 1024), BF16), jax.ShapeDtypeStruct((T, 640), F32),
                   jax.ShapeDtypeStruct((T, 128), F32), jax.ShapeDtypeStruct((T, 2048), BF16),
                   jax.ShapeDtypeStruct((1024, T), BF16), jax.ShapeDtypeStruct((32, T), F32)],
        compiler_params=_cparams("parallel"),
        name="inproj",
    )(x2, g, wa, wc, wkg, wg, wkt, wgt, bkg, bgt)


def _split3(x):
    hi = x.astype(BF16)
    r = x - hi.astype(F32)
    mid = r.astype(BF16)
    lo = (r - mid.astype(F32)).astype(BF16)
    return hi, mid, lo


def _dot3(a_f32, b_bf16):
    hi, mid, lo = _split3(a_f32)
    return _dot(hi, b_bf16) + _dot(mid, b_bf16) + _dot(lo, b_bf16)


def _dot3r(a_bf16, b_f32):
    hi, mid, lo = _split3(b_f32)
    return _dot(a_bf16, hi) + _dot(a_bf16, mid) + _dot(a_bf16, lo)


def _cummax_rows(y, reverse):
    n = y.shape[0]
    row = lax.broadcasted_iota(jnp.int32, y.shape, 0)
    k = 1
    while k < n:
        if reverse:
            sh = jnp.where(row < n - k, pltpu.roll(y, n - k, 0), -jnp.inf)
        else:
            sh = jnp.where(row >= k, pltpu.roll(y, k, 0), -jnp.inf)
        y = jnp.maximum(y, sh)
        k *= 2
    return y


def _mlstm_kernel(*refs, reverse):
    if reverse:
        q_ref, kt_ref, v_ref, kg_ref, gt_ref, hf_ref, os_ref, ng_ref, out_ref, cn_scr, m_scr = refs
    else:
        q_ref, kt_ref, v_ref, kg_ref, gt_ref, out_ref, cn_scr, m_scr = refs
    L = A_CHUNK

    @pl.when(pl.program_id(1) == 0)
    def _():
        cn_scr[...] = jnp.zeros_like(cn_scr)
        m_scr[...] = jnp.zeros_like(m_scr)

    row = lax.broadcasted_iota(jnp.int32, (L, L), 0)
    col = lax.broadcasted_iota(jnp.int32, (L, L), 1)
    mask = (col >= row) if reverse else (col <= row)
    tri_c = jnp.where(mask, 1.0, 0.0).astype(BF16)
    tri_r = jnp.where((row >= col) if reverse else (row <= col), 1.0, 0.0).astype(BF16)

    g = kg_ref[:, 64:96]
    gal = kg_ref[:, 96:128]
    gt = gt_ref[...]
    b_al = _dot3r(tri_c, gal)
    b_rows = _dot3(gt, tri_r)
    y = jnp.concatenate([g - b_al, jnp.zeros((L, LANE - 4 * A_HEADS), F32)], axis=1)
    cm = _cummax_rows(y, reverse)[:, :4 * A_HEADS]
    bc3 = jnp.concatenate(_split3(jnp.concatenate([b_al, cm], axis=1)), axis=1)
    off = 16 if reverse else 0
    ones = jnp.ones((L, LANE), BF16)
    sel_r = lax.broadcasted_iota(jnp.int32, (6 * 4 * A_HEADS, 2 * LANE), 0)
    sel_c = lax.broadcasted_iota(jnp.int32, (6 * 4 * A_HEADS, 2 * LANE), 1)
    sel_r = sel_r % (8 * A_HEADS) - jnp.where(sel_c >= LANE, 4 * A_HEADS, 0)
    last = 0 if reverse else L - 1

    H = range(A_HEADS)
    sls = [slice(h * LANE, (h + 1) * LANE) for h in H]
    bc = [_dot(bc3, jnp.where(sel_r == off + h, 1.0, 0.0).astype(BF16)) for h in H]
    b_c = [x[:, :LANE] for x in bc]
    cm_c = [x[:, LANE:] for x in bc]
    qk = [_dot(q_ref[:, sls[h]], kt_ref[sls[h], :]) for h in H]
    cns = [cn_scr[h] for h in H]
    qc = [_dot(q_ref[:, sls[h]], cns[h].astype(BF16)) for h in H]
    ms = [m_scr[h] for h in H]
    i_r = [gt[off + h:off + h + 1, :] for h in H]
    b_r = [b_rows[off + 8 + h:off + 9 + h, :] for h in H]
    m_t = [b_c[h] + jnp.maximum(cm_c[h], ms[h]) for h in H]
    s = [(qk[h] * jnp.where(mask, jnp.exp(b_c[h] - b_r[h] + i_r[h] - m_t[h]), 0.0)).astype(BF16) for h in H]
    vext = [jnp.concatenate([v_ref[:, sls[h]], ones], axis=1) for h in H]
    sv = [_dot(s[h], vext[h]) for h in H]
    b_last = [b_c[h][last:last + 1] for h in H]
    m_new = [b_last[h] + jnp.maximum(ms[h], cm_c[h][last:last + 1]) for h in H]
    kts = [(kt_ref[sls[h], :].astype(F32) * jnp.exp(b_last[h] - b_r[h] + i_r[h] - m_new[h])).astype(BF16) for h in H]
    kv = [_dot(kts[h], vext[h]) for h in H]
    for h in H:
        decay = jnp.exp(b_last[h] + ms[h] - m_new[h])
        cn_scr[h] = jnp.concatenate([decay, decay], axis=1) * cns[h] + kv[h]
        m_scr[h] = m_new[h]
        inter_w = jnp.exp(b_c[h] + ms[h] - m_t[h])
        tot = sv[h] + jnp.concatenate([inter_w, inter_w], axis=1) * qc[h]
        hh = tot[:, :LANE] / jnp.maximum(jnp.abs(tot[:, LANE:]), jnp.exp(-m_t[h]))
        if reverse:
            t = hh + hf_ref[:, sls[h]]
            y = t * lax.rsqrt(jnp.mean(t * t, axis=-1, keepdims=True) + EPS) * ng_ref[:, sls[h]]
            out_ref[:, sls[h]] = (y * os_ref[:, sls[h]].astype(F32)).astype(BF16)
        else:
            out_ref[:, sls[h]] = hh


def _mlstm(qa, kt, va, kg, gt, B, S, reverse, hf=None, osig=None, ng=None):
    L = A_CHUNK
    nc = S // L
    T = B * S
    if reverse:
        blk = lambda b, c: b * nc + (nc - 1 - c)
    else:
        blk = lambda b, c: b * nc + c
    row = lambda w: pl.BlockSpec((L, w), lambda b, c: (blk(b, c), 0))
    col = lambda r: pl.BlockSpec((r, L), lambda b, c: (0, blk(b, c)))
    in_specs = [row(1024), col(1024), row(1024), row(128), col(32)]
    args = [qa, kt, va, kg, gt]
    if reverse:
        in_specs += [row(1024), row(1024), _const_spec(ng.shape)]
        args += [hf, osig, ng]
    return pl.pallas_call(
        functools.partial(_mlstm_kernel, reverse=reverse),
        grid=(B, nc),
        in_specs=in_specs,
        out_specs=row(1024),
        out_shape=jax.ShapeDtypeStruct((T, 1024), BF16 if reverse else F32),
        scratch_shapes=[pltpu.VMEM((A_HEADS, LANE, 2 * LANE), F32), pltpu.VMEM((A_HEADS, 1, LANE), F32)],
        compiler_params=_cparams("parallel", "arbitrary"),
        name="mlstm_bwd" if reverse else "mlstm_fwd",
    )(*args)


def _rope(x, cos, sin_signed, axis):
    idx = lax.broadcasted_iota(jnp.int32, x.shape, axis)
    half = B_ROPE // 2
    rot = jnp.where(idx < half, pltpu.roll(x, LANE - half, axis), pltpu.roll(x, half, axis))
    return x * cos + rot * sin_signed


BOUND_COL = B_DQK - LANE
LOG2E = 1.4426950408889634


def _mlaprep_kernel(cqkv_ref, kg_ref, cos_ref, sin_ref, cost_ref, sint_ref, gq_ref, gkv_ref, wq_ref, wkt_ref,
                    wv_ref, nq_ref, nkt_ref, kb_ref, q_ref, kt_ref, v_ref):
    ts = cqkv_ref.shape[0]
    cq = _rms(cqkv_ref[:, :Q_LORA], gq_ref[...]).astype(BF16)
    ckv = _rms(cqkv_ref[:, Q_LORA:], gkv_ref[...]).astype(BF16)
    kgt = kg_ref[...].T
    rowi = lax.broadcasted_iota(jnp.int32, kgt.shape, 0)
    krt = jnp.where(rowi < B_ROPE, kgt, 0.0)
    ss_r = jnp.sum(krt * krt, axis=0, keepdims=True)
    nkt = nkt_ref[...]
    krt_rot = _rope(krt * nkt[LANE:], cost_ref[...], sint_ref[...], 0)
    one_row = jnp.where(rowi == BOUND_COL, 1.0, 0.0)
    nq = nq_ref[...]
    cos, sin = cos_ref[...], sin_ref[...]
    lane = lax.broadcasted_iota(jnp.int32, (ts, LANE), 1)
    scale = B_DQK ** -0.5 * LOG2E
    kb = kb_ref[...]
    ones = jnp.ones((ts, B_DV), BF16)
    for h in range(B_HEADS):
        q = _dot(cq, wq_ref[h])
        rq = lax.rsqrt(jnp.sum(q * q, axis=-1, keepdims=True) * (1.0 / B_DQK) + EPS) * scale
        qn = q * nq
        qnorm = jnp.sqrt(jnp.sum(qn * qn, axis=-1, keepdims=True)) * rq
        q_ref[0, h, :, :LANE] = (qn[:, :LANE] * rq).astype(BF16)
        q_rot = _rope(qn[:, LANE:], cos, sin, 1) * rq
        q_ref[0, h, :, LANE:] = jnp.where(lane == BOUND_COL, -(qnorm * kb), q_rot).astype(BF16)
        knt = _dot_nt(wkt_ref[h], ckv)
        rk = lax.rsqrt((jnp.sum(knt * knt, axis=0, keepdims=True) + ss_r) * (1.0 / B_DQK) + EPS)
        kt_ref[0, h, :LANE, :] = (knt * nkt[:LANE] * rk).astype(BF16)
        kt_ref[0, h, LANE:, :] = (krt_rot * rk + one_row).astype(BF16)
        v_ref[0, h, :, :B_DV] = _dot(ckv, wv_ref[h]).astype(BF16)
        v_ref[0, h, :, B_DV:] = ones


def _mlaprep(cqkv, kg, cos, sin, cost, sint, gq, gkv, wq, wkt, wv, nq, nkt, kb, B, S, ts):
    ns = S // ts
    row = lambda w: pl.BlockSpec((ts, w), lambda b, i: (b * ns + i, 0))
    pos = pl.BlockSpec((ts, LANE), lambda b, i: (i, 0))
    post = pl.BlockSpec((LANE, ts), lambda b, i: (0, i))
    consts = (gq, gkv, wq, wkt, wv, nq, nkt, kb)
    return pl.pallas_call(
        _mlaprep_kernel,
        grid=(B, ns),
        in_specs=[row(640), row(128), pos, pos, post, post] + [_const_spec(a.shape) for a in consts],
        out_specs=[pl.BlockSpec((1, B_HEADS, ts, QK_PAD), lambda b, i: (b, 0, i, 0)),
                   pl.BlockSpec((1, B_HEADS, QK_PAD, ts), lambda b, i: (b, 0, 0, i)),
                   pl.BlockSpec((1, B_HEADS, ts, 2 * B_DV), lambda b, i: (b, 0, i, 0))],
        out_shape=[jax.ShapeDtypeStruct((B, B_HEADS, S, QK_PAD), BF16),
                   jax.ShapeDtypeStruct((B, B_HEADS, QK_PAD, S), BF16),
                   jax.ShapeDtypeStruct((B, B_HEADS, S, 2 * B_DV), BF16)],
        compiler_params=_cparams("parallel", "parallel"),
        name="mlaprep",
    )(cqkv, kg, cos, sin, cost, sint, *consts)


def _attn_kernel(q_ref, kt_ref, v_ref, o_ref, *, tk, nk):
    q = q_ref[0, 0]
    tq = q.shape[0]

    def body(j, carry):
        m, acc = carry
        start = pl.multiple_of(j * tk, tk)
        kt = kt_ref[0, 0, :, pl.ds(start, tk)]
        v = v_ref[0, 0, pl.ds(start, tk), :]
        s = _dot(q, kt)
        m_new = jnp.maximum(m, jnp.max(s, axis=1, keepdims=True))
        p = jnp.exp2(s - m_new)
        acc = jnp.exp2(m - m_new) * acc + _dot(p.astype(BF16), v)
        return m_new, acc

    init = (jnp.full((tq, 1), -jnp.inf, F32), jnp.zeros((tq, 2 * B_DV), F32))
    _, acc = lax.fori_loop(0, nk, body, init)
    o_ref[0] = (acc[:, :B_DV] / acc[:, B_DV:]).astype(BF16)


def _attn_fast_kernel(q_ref, kt_ref, v_ref, o_ref, *, tk, nk, unroll):
    q = q_ref[0, 0]
    tq = q.shape[0]

    def probs(j):
        kt = kt_ref[0, 0, :, pl.ds(pl.multiple_of(j * tk, tk), tk)]
        return jnp.exp2(_dot(q, kt)).astype(BF16)

    def weighted(j, p, acc):
        v = v_ref[0, 0, pl.ds(pl.multiple_of(j * tk, tk), tk), :]
        return acc + _dot(p, v)

    def body(j, carry):
        p, acc = carry
        return probs(j), weighted(j - 1, p, acc)

    p, acc = lax.fori_loop(1, nk, body, (probs(0), jnp.zeros((tq, 2 * B_DV), F32)), unroll=unroll)
    acc = weighted(nk - 1, p, acc)
    o_ref[0] = (acc[:, :B_DV] / acc[:, B_DV:]).astype(BF16)


def _attention(q, kt, v, tq, tk, fast):
    B, H, S, _ = q.shape
    nk = S // tk
    if fast:
        body, name = functools.partial(_attn_fast_kernel, tk=tk, nk=nk, unroll=max(1, min(8, nk - 1))), "attention_fast"
    else:
        body, name = functools.partial(_attn_kernel, tk=tk, nk=nk), "attention"
    return pl.pallas_call(
        body,
        grid=(B, H, S // tq),
        in_specs=[pl.BlockSpec((1, 1, tq, QK_PAD), lambda b, h, i: (b, h, i, 0)),
                  pl.BlockSpec((1, 1, QK_PAD, S), lambda b, h, i: (b, h, 0, 0)),
                  pl.BlockSpec((1, 1, S, 2 * B_DV), lambda b, h, i: (b, h, 0, 0))],
        out_specs=pl.BlockSpec((1, tq, B_DV), lambda b, h, i: (b, i, h)),
        out_shape=jax.ShapeDtypeStruct((B, S, H * B_DV), BF16),
        compiler_params=_cparams("parallel", "parallel", "arbitrary"),
        name=name,
    )(q, kt, v)


def _merge_kernel(ha_ref, ao_ref, gs_ref, x_ref, wpa_ref, wpb_ref, wo_ref, o_ref):
    ya = _dot(ha_ref[...], wpa_ref[...])
    yb = _dot(ao_ref[...], wpb_ref[...])
    merged = gs_ref[:, :D_MODEL].astype(F32) * ya + gs_ref[:, D_MODEL:].astype(F32) * yb
    o_ref[...] = x_ref[...] + _dot(merged.astype(BF16), wo_ref[...])


def _merge(ha, ao, gs, x2, wpa, wpb, wo, tm):
    T = x2.shape[0]
    row = lambda w: pl.BlockSpec((tm, w), lambda i: (i, 0))
    return pl.pallas_call(
        _merge_kernel,
        grid=(T // tm,),
        in_specs=[row(1024), row(1024), row(2048), row(1024)] + [_const_spec(w.shape) for w in (wpa, wpb, wo)],
        out_specs=row(1024),
        out_shape=jax.ShapeDtypeStruct((T, D_MODEL), F32),
        compiler_params=_cparams("parallel"),
        name="merge",
    )(ha, ao, gs, x2, wpa, wpb, wo)


N_EXTRACT = PEER_TOPK + 1
ROUTE_HEADS_PER_ITER = 4


def _top_rows(s, n):
    rows = []
    for _ in range(n):
        m = jnp.max(s, axis=0, keepdims=True)
        rows.append(m)
        s = jnp.where(s == m, -jnp.inf, s)
    return rows


def _stack8(rows, t):
    ri = lax.broadcasted_iota(jnp.int32, (8, t), 0)
    out = jnp.full((8, t), -jnp.inf, F32)
    for k, r in enumerate(rows):
        out = jnp.where(ri == k, r, out)
    return out


def _route_head(h, q_scr, k1, k2, p1_ref, p2_ref, th_ref):
    qh = q_scr[h]
    t = qh.shape[0]
    s1 = _dot_nt(k1, qh[:, :PEER_HALF])
    s2 = _dot_nt(k2, qh[:, PEER_HALF:])
    v1 = _top_rows(s1, N_EXTRACT)
    v2 = _top_rows(s2, N_EXTRACT)
    v2a, v2b, v2c = _stack8(v2[0:8], t), _stack8(v2[8:16], t), _stack8(v2[16:17], t)
    v1b, v1c = _stack8(v1[8:16], t), _stack8(v1[16:17], t)
    cand = jnp.concatenate([v1[i] + v2a for i in range(8)] +
                           [v1[0] + v2b, v1[0] + v2c, v1b + v2[0], v1c + v2[0]], axis=0)
    best = _top_rows(cand, N_EXTRACT)
    z = jnp.zeros_like(best[0])
    for b in best[:PEER_TOPK]:
        z = z + jnp.exp(b - best[0])
    theta = 0.5 * (best[PEER_TOPK - 1] + best[PEER_TOPK])
    m1, m2 = v1[0], v2[0]
    c = RSQRT2 / z
    p1_ref[h] = jnp.exp(s1 - m1) * c
    p2_ref[h] = jnp.exp(s2 - m2)
    th_ref[pl.ds(h, 1), :] = jnp.exp(theta - m1 - m2) * c


def _route_kernel(x_ref, g_ref, wq_ref, k1_ref, k2_ref, xt_ref, p1_ref, p2_ref, th_ref, q_scr):
    xn = _rms(x_ref[...], g_ref[...])
    xt_ref[...] = xn.T.astype(BF16)
    q = _dot(xn.astype(BF16), wq_ref[...])
    for h in range(PEER_HEADS):
        q_scr[h] = q[:, h * PEER_DQ:(h + 1) * PEER_DQ].astype(BF16)
    k1, k2 = k1_ref[...], k2_ref[...]

    def head_group(i, carry):
        for k in range(ROUTE_HEADS_PER_ITER):
            _route_head(ROUTE_HEADS_PER_ITER * i + k, q_scr, k1, k2, p1_ref, p2_ref, th_ref)
        return carry

    lax.fori_loop(0, PEER_HEADS // ROUTE_HEADS_PER_ITER, head_group, 0)


def _route(x1, g, wq, k1, k2, tr):
    T = x1.shape[0]
    return pl.pallas_call(
        _route_kernel,
        grid=(T // tr,),
        in_specs=[pl.BlockSpec((tr, D_MODEL), lambda i: (i, 0)), _const_spec(g.shape), _const_spec(wq.shape),
                  _const_spec(k1.shape), _const_spec(k2.shape)],
        out_specs=[pl.BlockSpec((D_MODEL, tr), lambda i: (0, i)),
                   pl.BlockSpec((PEER_HEADS, PEER_NKEYS, tr), lambda i: (0, 0, i)),
                   pl.BlockSpec((PEER_HEADS, PEER_NKEYS, tr), lambda i: (0, 0, i)),
                   pl.BlockSpec((PEER_HEADS, tr), lambda i: (0, i))],
        out_shape=[jax.ShapeDtypeStruct((D_MODEL, T), BF16),
                   jax.ShapeDtypeStruct((PEER_HEADS, PEER_NKEYS, T), F32),
                   jax.ShapeDtypeStruct((PEER_HEADS, PEER_NKEYS, T), F32),
                   jax.ShapeDtypeStruct((PEER_HEADS, T), F32)],
        scratch_shapes=[pltpu.VMEM((PEER_HEADS, tr, PEER_DQ), BF16)],
        compiler_params=_cparams("parallel"),
        name="route",
    )(x1, g, wq, k1, k2)


E1_PER_STEP = 16
RSQRT2 = 0.7071067811865476
PEER_TB = 256
PEER_GROUP = 2


def _peer_kernel(xt_ref, u_ref, vt_ref, p1_ref, p2_ref, th_ref, x1_ref, o_ref, acc_scr, g_scr):
    s = pl.program_id(0)
    nblk = PEER_EXPERTS // (E1_PER_STEP * PEER_NKEYS)
    G, TB = PEER_GROUP, PEER_TB

    @pl.when(s == 0)
    def _():
        acc_scr[...] = jnp.zeros_like(acc_scr)
        g_scr[...] = jnp.zeros_like(g_scr)

    t2 = jnp.maximum(s - 2, 0)

    @pl.when((s > 1) & (lax.rem(t2 // G, nblk) == nblk - 1))
    def _():
        f = lax.rem(t2, G)
        rows = pl.ds(pl.multiple_of(f * TB, TB), TB)
        o_ref[rows, :] = x1_ref[rows, :] + acc_scr[f].T
        acc_scr[f] = jnp.zeros((D_MODEL, TB), F32)

    slot = lax.rem(s, 2)
    acc_scr[lax.rem(s + G - 1, G)] += _dot(vt_ref[...], g_scr[1 - slot])
    cols = pl.ds(pl.multiple_of(lax.rem(s, G) * TB, TB), TB)
    xt = xt_ref[:, cols]
    for e in range(E1_PER_STEP):
        sl = slice(e * PEER_NKEYS, (e + 1) * PEER_NKEYS)
        a = _dot(u_ref[sl, :], xt)
        act = a * (1.0 + lax.erf(a))
        w = jnp.zeros_like(a)
        for h in range(PEER_HEADS):
            p = p2_ref[h, :, cols] * p1_ref[h, e:e + 1, cols]
            w = w + jnp.where(p >= th_ref[h:h + 1, cols], p, 0.0)
        g_scr[slot, sl, :] = (w * act).astype(BF16)


def _peer(xt, u, vt, p1, p2, th, x1):
    T = x1.shape[0]
    G, TB = PEER_GROUP, PEER_TB
    eb = E1_PER_STEP * PEER_NKEYS
    nblk = PEER_EXPERTS // eb
    per = nblk * G
    ng = T // (G * TB)
    grp = lambda s: jnp.minimum(s // per, ng - 1)
    done = lambda s: jnp.maximum(s - 2, 0) // per
    return pl.pallas_call(
        _peer_kernel,
        grid=(ng * per + 2,),
        in_specs=[pl.BlockSpec((D_MODEL, G * TB), lambda s: (0, grp(s))),
                  pl.BlockSpec((eb, D_MODEL), lambda s: ((s // G) % nblk, 0)),
                  pl.BlockSpec((D_MODEL, eb), lambda s: (0, (jnp.maximum(s - 1, 0) // G) % nblk)),
                  pl.BlockSpec((PEER_HEADS, E1_PER_STEP, G * TB), lambda s: (0, (s // G) % nblk, grp(s))),
                  pl.BlockSpec((PEER_HEADS, PEER_NKEYS, G * TB), lambda s: (0, 0, grp(s))),
                  pl.BlockSpec((PEER_HEADS, G * TB), lambda s: (0, grp(s))),
                  pl.BlockSpec((G * TB, D_MODEL), lambda s: (done(s), 0))],
        out_specs=pl.BlockSpec((G * TB, D_MODEL), lambda s: (done(s), 0)),
        out_shape=jax.ShapeDtypeStruct((T, D_MODEL), F32),
        scratch_shapes=[pltpu.VMEM((G, D_MODEL, TB), F32), pltpu.VMEM((2, eb, TB), BF16)],
        compiler_params=_cparams("arbitrary"),
        name="peer",
    )(xt, u, vt, p1, p2, th, x1)


def _prep_weights(p):
    w_in = p["w_in"]
    sizes = (A_HEADS * A_DQK, A_HEADS * A_DQK, A_HEADS * A_DV, A_HEADS * A_DV, 4 * A_HEADS,
             Q_LORA, KV_LORA, B_ROPE, 2 * D_MODEL)
    pts = np.cumsum((0,) + sizes)
    w_q, w_k, w_v, w_o, w_gate, w_cq, w_ckv, w_kr, w_gbr = (w_in[:, pts[i]:pts[i + 1]] for i in range(9))

    def pad_heads(w):
        w = w.reshape(D_MODEL, A_HEADS, A_DQK)
        return jnp.pad(w, ((0, 0), (0, 0), (0, LANE - A_DQK))).reshape(D_MODEL, A_HEADS * LANE)

    out = {}
    out["wa"] = jnp.concatenate([pad_heads(w_q) * (A_DQK ** -0.5), w_v, w_o], axis=1).astype(BF16)
    out["wc"] = jnp.concatenate([w_cq, w_ckv], axis=1).astype(BF16)
    wg4 = w_gate.reshape(D_MODEL, 4, A_HEADS)
    w_fal = jnp.stack([wg4[:, 1], wg4[:, 1], wg4[:, 3], wg4[:, 3]], axis=1).reshape(D_MODEL, 4 * A_HEADS)
    out["wkg"] = jnp.concatenate([w_kr, w_gate, w_fal], axis=1).astype(BF16)
    out["wg"] = w_gbr.astype(BF16)
    out["wkt"] = pad_heads(w_k).T.astype(BF16)
    out["wgt"] = w_gate.T.astype(BF16)
    bias = p["b_mgate"].astype(F32).reshape(4 * A_HEADS)
    b4 = bias.reshape(4, A_HEADS)
    b_fal = jnp.stack([b4[1], b4[1], b4[3], b4[3]]).reshape(4 * A_HEADS)
    out["bkg"] = jnp.concatenate([jnp.zeros((B_ROPE,), F32), bias, b_fal]).reshape(1, LANE)
    out["bgt"] = bias.reshape(4 * A_HEADS, 1)
    out["g_mix"] = p["norm_mix_g"].astype(F32).reshape(1, D_MODEL)
    out["g_mlstm"] = p["mlstm_norm_g"].astype(F32).reshape(1, A_HEADS * A_DV)
    out["gq"] = p["q_a_norm_g"].astype(F32).reshape(1, Q_LORA)
    out["gkv"] = p["kv_a_norm_g"].astype(F32).reshape(1, KV_LORA)
    wq = p["w_q_up"].reshape(Q_LORA, B_HEADS, B_DQK).transpose(1, 0, 2)
    out["wq"] = jnp.pad(wq, ((0, 0), (0, 0), (0, QK_PAD - B_DQK))).astype(BF16)
    wkv = p["w_kv_up"].reshape(KV_LORA, B_HEADS, B_NOPE + B_DV).transpose(1, 0, 2)
    out["wkbt"] = wkv[:, :, :B_NOPE].transpose(0, 2, 1).astype(BF16)
    out["wv"] = wkv[:, :, B_NOPE:].astype(BF16)
    out["nq"] = jnp.pad(p["qk_norm_q_g"].astype(F32), (0, QK_PAD - B_DQK)).reshape(1, QK_PAD)
    nk = p["qk_norm_k_g"].astype(F32)
    out["nk_col"] = jnp.pad(nk, (0, QK_PAD - B_DQK)).reshape(QK_PAD, 1)
    out["kb"] = (1.01 * B_DQK ** 0.5 * jnp.max(jnp.abs(nk))).reshape(1, 1)
    out["score_bound"] = 1.01 * B_DQK ** 0.5 * LOG2E * jnp.max(jnp.abs(nk)) * jnp.max(jnp.abs(p["qk_norm_q_g"]))
    out["wpa"] = p["w_proj_a"].astype(BF16)
    out["wpb"] = p["w_proj_b"].astype(BF16)
    out["wo"] = p["w_out"].astype(BF16)
    out["g_ffn"] = p["norm_ffn_g"].astype(F32).reshape(1, D_MODEL)
    out["wpq"] = p["w_peer_q"].astype(BF16)
    out["k1"] = p["peer_keys1"].astype(BF16)
    out["k2"] = p["peer_keys2"].astype(BF16)
    out["u"] = (p["peer_u"] * RSQRT2).astype(BF16)
    out["vt"] = p["peer_v"].T.astype(BF16)
    return out


def _rope_tables(S):
    pos = jnp.arange(S, dtype=F32)
    inv = ROPE_THETA ** (-jnp.arange(0, B_ROPE, 2, dtype=F32) / B_ROPE)
    ang = pos[:, None] * inv[None, :]
    ang = jnp.concatenate([ang, ang], axis=-1)
    sign = jnp.where(jnp.arange(B_ROPE) < B_ROPE // 2, -1.0, 1.0).astype(F32)
    pad = ((0, 0), (0, LANE - B_ROPE))
    cos, sin = jnp.pad(jnp.cos(ang), pad), jnp.pad(jnp.sin(ang) * sign, pad)
    return cos, sin, cos.T, sin.T


def _pick(n, pref):
    t = min(n, pref)
    assert n % t == 0, (n, t)
    return t


def _layer(x, w):
    B, S, _ = x.shape
    T = B * S
    x2 = x.reshape(T, D_MODEL)
    qa, va, osig, cqkv, kg, gs, kt, gt = _inproj(
        x2, w["g_mix"], w["wa"], w["wc"], w["wkg"], w["wg"], w["wkt"], w["wgt"], w["bkg"], w["bgt"],
        _pick(T, 256))
    hf = _mlstm(qa, kt, va, kg, gt, B, S, False)
    ha = _mlstm(qa, kt, va, kg, gt, B, S, True, hf, osig, w["g_mlstm"])
    cos, sin, cost, sint = _rope_tables(S)
    ts = _pick(S, 256)
    nkt = jnp.broadcast_to(w["nk_col"], (QK_PAD, ts))
    q, kbt, v = _mlaprep(cqkv, kg, cos, sin, cost, sint, w["gq"], w["gkv"], w["wq"], w["wkbt"], w["wv"], w["nq"],
                         nkt, w["kb"], B, S, ts)
    tq, tk = _pick(S, 512), _pick(S, 512)
    ao = lax.cond(w["score_bound"] <= 50.0,
                  lambda *a: _attention(*a, tq, tk, True),
                  lambda *a: _attention(*a, tq, tk, False),
                  q, kbt, v).reshape(T, B_HEADS * B_DV)
    x1 = _merge(ha, ao, gs, x2, w["wpa"], w["wpb"], w["wo"], _pick(T, 512))
    xt, p1, p2, th = _route(x1, w["g_ffn"], w["wpq"], w["k1"], w["k2"], _pick(T, 256))
    y = _peer(xt, w["u"], w["vt"], p1, p2, th, x1)
    return y.reshape(B, S, D_MODEL)


def kernel(x_prompt, x_sample, norm_mix_g, w_in, b_mgate, mlstm_norm_g, q_a_norm_g, w_q_up, kv_a_norm_g, w_kv_up, qk_norm_q_g, qk_norm_k_g, w_proj_a, w_proj_b, w_out, norm_ffn_g, w_peer_q, peer_keys1, peer_keys2, peer_u, peer_v):
    params = dict(norm_mix_g=norm_mix_g, w_in=w_in, b_mgate=b_mgate, mlstm_norm_g=mlstm_norm_g,
                  q_a_norm_g=q_a_norm_g, w_q_up=w_q_up, kv_a_norm_g=kv_a_norm_g, w_kv_up=w_kv_up,
                  qk_norm_q_g=qk_norm_q_g, qk_norm_k_g=qk_norm_k_g, w_proj_a=w_proj_a, w_proj_b=w_proj_b,
                  w_out=w_out, norm_ffn_g=norm_ffn_g, w_peer_q=w_peer_q, peer_keys1=peer_keys1,
                  peer_keys2=peer_keys2, peer_u=peer_u, peer_v=peer_v)
    depth = w_in.shape[0]
    layers = [_prep_weights({k: v[l] for k, v in params.items()}) for l in range(depth)]

    def run(x):
        for w in layers:
            x = _layer(x, w)
        return x

    return (run(x_prompt), run(x_sample))
```

```python
import functools

import jax
import jax.numpy as jnp
import numpy as np
from jax import lax
from jax.experimental import pallas as pl
from jax.experimental.pallas import tpu as pltpu

F32 = jnp.float32
BF16 = jnp.bfloat16

D_MODEL = 1024
A_HEADS, A_DQK, A_DV, A_CHUNK = 8, 64, 128, 128
B_HEADS, B_NOPE, B_ROPE, B_DV = 8, 128, 64, 128
B_DQK = B_NOPE + B_ROPE
Q_LORA, KV_LORA = 384, 256
ROPE_THETA = 10000.0
PEER_HEADS, PEER_NKEYS, PEER_DQ, PEER_TOPK = 8, 128, 256, 16
PEER_HALF = PEER_DQ // 2
PEER_EXPERTS = PEER_NKEYS * PEER_NKEYS
EPS = 1e-6

LANE = 128
QK_PAD = 2 * LANE
VMEM_LIMIT = 52 * 1024 * 1024


def _cparams(*sem):
    return pltpu.CompilerParams(dimension_semantics=sem, vmem_limit_bytes=VMEM_LIMIT)


def _dot(a, b):
    return jnp.dot(a, b, preferred_element_type=F32)


def _dot_nt(a, b):
    return lax.dot_general(a, b, (((1,), (1,)), ((), ())), preferred_element_type=F32)


def _rms(x, g):
    return x * lax.rsqrt(jnp.mean(x * x, axis=-1, keepdims=True) + EPS) * g


def _log_sigmoid(x):
    return jnp.minimum(x, 0.0) - jnp.log(1.0 + jnp.exp(-jnp.abs(x)))


def _sigmoid(x):
    return 1.0 / (1.0 + jnp.exp(-x))


def _const_spec(shape):
    nd = len(shape)
    return pl.BlockSpec(shape, lambda *_: (0,) * nd)


def _split3(x):
    hi = x.astype(BF16)
    r = x - hi.astype(F32)
    mid = r.astype(BF16)
    lo = (r - mid.astype(F32)).astype(BF16)
    return hi, mid, lo


def _dot3(a_f32, b_bf16):
    hi, mid, lo = _split3(a_f32)
    return _dot(hi, b_bf16) + _dot(mid, b_bf16) + _dot(lo, b_bf16)


def _dot3r(a_bf16, b_f32):
    hi, mid, lo = _split3(b_f32)
    return _dot(a_bf16, hi) + _dot(a_bf16, mid) + _dot(a_bf16, lo)


def _cummax_rows(y, reverse, period):
    n = y.shape[0]
    row = lax.broadcasted_iota(jnp.int32, y.shape, 0) % period
    k = 1
    while k < period:
        if reverse:
            sh = jnp.where(row < period - k, pltpu.roll(y, n - k, 0), -jnp.inf)
        else:
            sh = jnp.where(row >= k, pltpu.roll(y, k, 0), -jnp.inf)
        y = jnp.maximum(y, sh)
        k *= 2
    return y


def _inproj_kernel(x_ref, g_ref, wa_ref, wc_ref, wkg_ref, wg_ref, wkt_ref, wgt_ref, bkg_ref, bgt_ref,
                   qa_ref, va_ref, os_ref, cqkv_ref, kg_ref, gs_ref, kt_ref, gt_ref, kb_ref, bt_ref):
    h = _rms(x_ref[...], g_ref[...]).astype(BF16)
    a = _dot(h, wa_ref[...])
    qa_ref[...] = a[:, :1024].astype(BF16)
    va_ref[...] = a[:, 1024:2048].astype(BF16)
    os_ref[...] = _sigmoid(a[:, 2048:]).astype(BF16)
    cqkv_ref[...] = _dot(h, wc_ref[...])
    kg = _dot(h, wkg_ref[...]) + bkg_ref[...]
    lane = lax.broadcasted_iota(jnp.int32, kg.shape, 1)
    is_f = ((lane >= 72) & (lane < 80)) | (lane >= 88)
    kg = jnp.where(is_f, _log_sigmoid(kg), kg)
    kg_ref[...] = kg
    gs_ref[...] = _sigmoid(_dot(h, wg_ref[...])).astype(BF16)
    kt_ref[...] = _dot_nt(wkt_ref[...], h).astype(BF16)
    gt = _dot_nt(wgt_ref[...], h) + bgt_ref[...]
    row = lax.broadcasted_iota(jnp.int32, gt.shape, 0)
    is_fr = ((row >= 8) & (row < 16)) | (row >= 24)
    gt = jnp.where(is_fr, _log_sigmoid(gt), gt)
    gt_ref[...] = gt
    tm = kg.shape[0]
    r = lax.broadcasted_iota(jnp.int32, (tm, tm), 0)
    c = lax.broadcasted_iota(jnp.int32, (tm, tm), 1)
    same = (r // A_CHUNK) == (c // A_CHUNK)
    lower = jnp.where(same & (c <= r), 1.0, 0.0).astype(BF16)
    upper = jnp.where(same & (c >= r), 1.0, 0.0).astype(BF16)
    g, gal = kg[:, 64:96], kg[:, 96:128]
    fwd_l = lax.broadcasted_iota(jnp.int32, g.shape, 1) < 2 * A_HEADS
    b_al = jnp.where(fwd_l, _dot3r(lower, gal), _dot3r(upper, gal))
    y = jnp.concatenate([g - b_al, jnp.zeros((tm, LANE - 4 * A_HEADS), F32)], axis=1)
    cm = jnp.where(fwd_l, _cummax_rows(y, False, A_CHUNK)[:, :4 * A_HEADS],
                   _cummax_rows(y, True, A_CHUNK)[:, :4 * A_HEADS])
    kb_ref[...] = jnp.concatenate([b_al, cm, jnp.zeros((tm, LANE - 8 * A_HEADS), F32)], axis=1)
    fwd_r = lax.broadcasted_iota(jnp.int32, gt.shape, 0) < 2 * A_HEADS
    bt_ref[...] = jnp.where(fwd_r, _dot3(gt, upper), _dot3(gt, lower))


def _inproj(x2, g, wa, wc, wkg, wg, wkt, wgt, bkg, bgt, tm):
    T = x2.shape[0]
    row = lambda w: pl.BlockSpec((tm, w), lambda i: (i, 0))
    col = lambda r: pl.BlockSpec((r, tm), lambda i: (0, i))
    return pl.pallas_call(
        _inproj_kernel,
        grid=(T // tm,),
        in_specs=[row(D_MODEL), _const_spec(g.shape), _const_spec(wa.shape), _const_spec(wc.shape),
                  _const_spec(wkg.shape), _const_spec(wg.shape), _const_spec(wkt.shape),
                  _const_spec(wgt.shape), _const_spec(bkg.shape), _const_spec(bgt.shape)],
        out_specs=[row(1024), row(1024), row(1024), row(640), row(128), row(2048), col(1024), col(32), row(128),
                   col(32)],
        out_shape=[jax.ShapeDtypeStruct((T, 1024), BF16), jax.ShapeDtypeStruct((T, 1024), BF16),
                   jax.ShapeDtypeStruct((T, 1024), BF16), jax.ShapeDtypeStruct((T, 640), F32),
                   jax.ShapeDtypeStruct((T, 128), F32), jax.ShapeDtypeStruct((T, 2048), BF16),
                   jax.ShapeDtypeStruct((1024, T), BF16), jax.ShapeDtypeStruct((32, T), F32),
                   jax.ShapeDtypeStruct((T, 128), F32), jax.ShapeDtypeStruct((32, T), F32)],
        compiler_params=_cparams("parallel"),
        name="inproj",
    )(x2, g, wa, wc, wkg, wg, wkt, wgt, bkg, bgt)


def _mlstm_kernel(*refs, reverse):
    if reverse:
        q_ref, kt_ref, v_ref, kb_ref, gt_ref, bt_ref, hf_ref, os_ref, ng_ref, out_ref, cn_scr, m_scr = refs
    else:
        q_ref, kt_ref, v_ref, kb_ref, gt_ref, bt_ref, out_ref, cn_scr, m_scr = refs
    L = A_CHUNK

    @pl.when(pl.program_id(1) == 0)
    def _():
        cn_scr[...] = jnp.zeros_like(cn_scr)
        m_scr[...] = jnp.zeros_like(m_scr)

    row = lax.broadcasted_iota(jnp.int32, (L, L), 0)
    col = lax.broadcasted_iota(jnp.int32, (L, L), 1)
    mask = (col >= row) if reverse else (col <= row)
    gt = gt_ref[...]
    b_rows = bt_ref[...]
    bc3 = jnp.concatenate(_split3(kb_ref[:, :8 * A_HEADS]), axis=1)
    off = 16 if reverse else 0
    ones = jnp.ones((L, LANE), BF16)
    sel_r = lax.broadcasted_iota(jnp.int32, (6 * 4 * A_HEADS, 2 * LANE), 0)
    sel_c = lax.broadcasted_iota(jnp.int32, (6 * 4 * A_HEADS, 2 * LANE), 1)
    sel_r = sel_r % (8 * A_HEADS) - jnp.where(sel_c >= LANE, 4 * A_HEADS, 0)
    last = 0 if reverse else L - 1

    H = range(A_HEADS)
    sls = [slice(h * LANE, (h + 1) * LANE) for h in H]
    bc = [_dot(bc3, jnp.where(sel_r == off + h, 1.0, 0.0).astype(BF16)) for h in H]
    b_c = [x[:, :LANE] for x in bc]
    cm_c = [x[:, LANE:] for x in bc]
    qk = [_dot(q_ref[:, sls[h]], kt_ref[sls[h], :]) for h in H]
    cns = [cn_scr[h] for h in H]
    qc = [_dot(q_ref[:, sls[h]], cns[h].astype(BF16)) for h in H]
    ms = [m_scr[h] for h in H]
    i_r = [gt[off + h:off + h + 1, :] for h in H]
    b_r = [b_rows[off + 8 + h:off + 9 + h, :] for h in H]
    m_t = [b_c[h] + jnp.maximum(cm_c[h], ms[h]) for h in H]
    s = [(qk[h] * jnp.where(mask, jnp.exp(b_c[h] - b_r[h] + i_r[h] - m_t[h]), 0.0)).astype(BF16) for h in H]
    vext = [jnp.concatenate([v_ref[:, sls[h]], ones], axis=1) for h in H]
    sv = [_dot(s[h], vext[h]) for h in H]
    b_last = [b_c[h][last:last + 1] for h in H]
    m_new = [b_last[h] + jnp.maximum(ms[h], cm_c[h][last:last + 1]) for h in H]
    kts = [(kt_ref[sls[h], :].astype(F32) * jnp.exp(b_last[h] - b_r[h] + i_r[h] - m_new[h])).astype(BF16) for h in H]
    kv = [_dot(kts[h], vext[h]) for h in H]
    for h in H:
        decay = jnp.exp(b_last[h] + ms[h] - m_new[h])
        cn_scr[h] = jnp.concatenate([decay, decay], axis=1) * cns[h] + kv[h]
        m_scr[h] = m_new[h]
        inter_w = jnp.exp(b_c[h] + ms[h] - m_t[h])
        tot = sv[h] + jnp.concatenate([inter_w, inter_w], axis=1) * qc[h]
        hh = tot[:, :LANE] / jnp.maximum(jnp.abs(tot[:, LANE:]), jnp.exp(-m_t[h]))
        if reverse:
            t = hh + hf_ref[:, sls[h]]
            y = t * lax.rsqrt(jnp.mean(t * t, axis=-1, keepdims=True) + EPS) * ng_ref[:, sls[h]]
            out_ref[:, sls[h]] = (y * os_ref[:, sls[h]].astype(F32)).astype(BF16)
        else:
            out_ref[:, sls[h]] = hh


def _mlstm(qa, kt, va, kb, gt, bt, B, S, reverse, hf=None, osig=None, ng=None):
    L = A_CHUNK
    nc = S // L
    T = B * S
    if reverse:
        blk = lambda b, c: b * nc + (nc - 1 - c)
    else:
        blk = lambda b, c: b * nc + c
    row = lambda w: pl.BlockSpec((L, w), lambda b, c: (blk(b, c), 0))
    col = lambda r: pl.BlockSpec((r, L), lambda b, c: (0, blk(b, c)))
    in_specs = [row(1024), col(1024), row(1024), row(128), col(32), col(32)]
    args = [qa, kt, va, kb, gt, bt]
    if reverse:
        in_specs += [row(1024), row(1024), _const_spec(ng.shape)]
        args += [hf, osig, ng]
    return pl.pallas_call(
        functools.partial(_mlstm_kernel, reverse=reverse),
        grid=(B, nc),
        in_specs=in_specs,
        out_specs=row(1024),
        out_shape=jax.ShapeDtypeStruct((T, 1024), BF16 if reverse else F32),
        scratch_shapes=[pltpu.VMEM((A_HEADS, LANE, 2 * LANE), F32), pltpu.VMEM((A_HEADS, 1, LANE), F32)],
        compiler_params=_cparams("parallel", "arbitrary"),
        name="mlstm_bwd" if reverse else "mlstm_fwd",
    )(*args)


def _rope(x, cos, sin_signed, axis):
    idx = lax.broadcasted_iota(jnp.int32, x.shape, axis)
    half = B_ROPE // 2
    rot = jnp.where(idx < half, pltpu.roll(x, LANE - half, axis), pltpu.roll(x, half, axis))
    return x * cos + rot * sin_signed


BOUND_COL = B_DQK - LANE
LOG2E = 1.4426950408889634


def _mlaprep_kernel(cqkv_ref, kg_ref, cos_ref, sin_ref, cost_ref, sint_ref, gq_ref, gkv_ref, wq_ref, wkt_ref,
                    wv_ref, nq_ref, nkt_ref, kb_ref, q_ref, kt_ref, v_ref):
    ts = cqkv_ref.shape[0]
    cq = _rms(cqkv_ref[:, :Q_LORA], gq_ref[...]).astype(BF16)
    ckv = _rms(cqkv_ref[:, Q_LORA:], gkv_ref[...]).astype(BF16)
    kgt = kg_ref[...].T
    rowi = lax.broadcasted_iota(jnp.int32, kgt.shape, 0)
    krt = jnp.where(rowi < B_ROPE, kgt, 0.0)
    ss_r = jnp.sum(krt * krt, axis=0, keepdims=True)
    nkt = nkt_ref[...]
    krt_rot = _rope(krt * nkt[LANE:], cost_ref[...], sint_ref[...], 0)
    one_row = jnp.where(rowi == BOUND_COL, 1.0, 0.0)
    nq = nq_ref[...]
    cos, sin = cos_ref[...], sin_ref[...]
    lane = lax.broadcasted_iota(jnp.int32, (ts, LANE), 1)
    scale = B_DQK ** -0.5 * LOG2E
    kb = kb_ref[...]
    ones = jnp.ones((ts, B_DV), BF16)
    for h in range(B_HEADS):
        q = _dot(cq, wq_ref[h])
        rq = lax.rsqrt(jnp.sum(q * q, axis=-1, keepdims=True) * (1.0 / B_DQK) + EPS) * scale
        qn = q * nq
        qnorm = jnp.sqrt(jnp.sum(qn * qn, axis=-1, keepdims=True)) * rq
        q_ref[0, h, :, :LANE] = (qn[:, :LANE] * rq).astype(BF16)
        q_rot = _rope(qn[:, LANE:], cos, sin, 1) * rq
        q_ref[0, h, :, LANE:] = jnp.where(lane == BOUND_COL, -(qnorm * kb), q_rot).astype(BF16)
        knt = _dot_nt(wkt_ref[h], ckv)
        rk = lax.rsqrt((jnp.sum(knt * knt, axis=0, keepdims=True) + ss_r) * (1.0 / B_DQK) + EPS)
        kt_ref[0, h, :LANE, :] = (knt * nkt[:LANE] * rk).astype(BF16)
        kt_ref[0, h, LANE:, :] = (krt_rot * rk + one_row).astype(BF16)
        v_ref[0, h, :, :B_DV] = _dot(ckv, wv_ref[h]).astype(BF16)
        v_ref[0, h, :, B_DV:] = ones


def _mlaprep(cqkv, kg, cos, sin, cost, sint, gq, gkv, wq, wkt, wv, nq, nkt, kb, B, S, ts):
    ns = S // ts
    row = lambda w: pl.BlockSpec((ts, w), lambda b, i: (b * ns + i, 0))
    pos = pl.BlockSpec((ts, LANE), lambda b, i: (i, 0))
    post = pl.BlockSpec((LANE, ts), lambda b, i: (0, i))
    consts = (gq, gkv, wq, wkt, wv, nq, nkt, kb)
    return pl.pallas_call(
        _mlaprep_kernel,
        grid=(B, ns),
        in_specs=[row(640), row(128), pos, pos, post, post] + [_const_spec(a.shape) for a in consts],
        out_specs=[pl.BlockSpec((1, B_HEADS, ts, QK_PAD), lambda b, i: (b, 0, i, 0)),
                   pl.BlockSpec((1, B_HEADS, QK_PAD, ts), lambda b, i: (b, 0, 0, i)),
                   pl.BlockSpec((1, B_HEADS, ts, 2 * B_DV), lambda b, i: (b, 0, i, 0))],
        out_shape=[jax.ShapeDtypeStruct((B, B_HEADS, S, QK_PAD), BF16),
                   jax.ShapeDtypeStruct((B, B_HEADS, QK_PAD, S), BF16),
                   jax.ShapeDtypeStruct((B, B_HEADS, S, 2 * B_DV), BF16)],
        compiler_params=_cparams("parallel", "parallel"),
        name="mlaprep",
    )(cqkv, kg, cos, sin, cost, sint, *consts)


def _attn_kernel(q_ref, kt_ref, v_ref, o_ref, *, tk, nk):
    q = q_ref[0, 0]
    tq = q.shape[0]

    def body(j, carry):
        m, acc = carry
        start = pl.multiple_of(j * tk, tk)
        kt = kt_ref[0, 0, :, pl.ds(start, tk)]
        v = v_ref[0, 0, pl.ds(start, tk), :]
        s = _dot(q, kt)
        m_new = jnp.maximum(m, jnp.max(s, axis=1, keepdims=True))
        p = jnp.exp2(s - m_new)
        acc = jnp.exp2(m - m_new) * acc + _dot(p.astype(BF16), v)
        return m_new, acc

    init = (jnp.full((tq, 1), -jnp.inf, F32), jnp.zeros((tq, 2 * B_DV), F32))
    _, acc = lax.fori_loop(0, nk, body, init)
    o_ref[0] = (acc[:, :B_DV] / acc[:, B_DV:]).astype(BF16)


def _attn_fast_kernel(q_ref, kt_ref, v_ref, o_ref, *, tk, nk, unroll):
    q = q_ref[0, 0]
    tq = q.shape[0]

    def probs(j):
        kt = kt_ref[0, 0, :, pl.ds(pl.multiple_of(j * tk, tk), tk)]
        return jnp.exp2(_dot(q, kt)).astype(BF16)

    def weighted(j, p, acc):
        v = v_ref[0, 0, pl.ds(pl.multiple_of(j * tk, tk), tk), :]
        return acc + _dot(p, v)

    def body(j, carry):
        p, acc = carry
        return probs(j), weighted(j - 1, p, acc)

    p, acc = lax.fori_loop(1, nk, body, (probs(0), jnp.zeros((tq, 2 * B_DV), F32)), unroll=unroll)
    acc = weighted(nk - 1, p, acc)
    o_ref[0] = (acc[:, :B_DV] / acc[:, B_DV:]).astype(BF16)


def _attention(q, kt, v, tq, tk, fast):
    B, H, S, _ = q.shape
    nk = S // tk
    if fast:
        body, name = functools.partial(_attn_fast_kernel, tk=tk, nk=nk, unroll=max(1, min(8, nk - 1))), "attention_fast"
    else:
        body, name = functools.partial(_attn_kernel, tk=tk, nk=nk), "attention"
    return pl.pallas_call(
        body,
        grid=(B, H, S // tq),
        in_specs=[pl.BlockSpec((1, 1, tq, QK_PAD), lambda b, h, i: (b, h, i, 0)),
                  pl.BlockSpec((1, 1, QK_PAD, S), lambda b, h, i: (b, h, 0, 0)),
                  pl.BlockSpec((1, 1, S, 2 * B_DV), lambda b, h, i: (b, h, 0, 0))],
        out_specs=pl.BlockSpec((1, tq, B_DV), lambda b, h, i: (b, i, h)),
        out_shape=jax.ShapeDtypeStruct((B, S, H * B_DV), BF16),
        compiler_params=_cparams("parallel", "parallel", "arbitrary"),
        name=name,
    )(q, kt, v)


def _merge_kernel(ha_ref, ao_ref, gs_ref, x_ref, wpa_ref, wpb_ref, wo_ref, o_ref):
    ya = _dot(ha_ref[...], wpa_ref[...])
    yb = _dot(ao_ref[...], wpb_ref[...])
    merged = gs_ref[:, :D_MODEL].astype(F32) * ya + gs_ref[:, D_MODEL:].astype(F32) * yb
    o_ref[...] = x_ref[...] + _dot(merged.astype(BF16), wo_ref[...])


def _merge(ha, ao, gs, x2, wpa, wpb, wo, tm):
    T = x2.shape[0]
    row = lambda w: pl.BlockSpec((tm, w), lambda i: (i, 0))
    return pl.pallas_call(
        _merge_kernel,
        grid=(T // tm,),
        in_specs=[row(1024), row(1024), row(2048), row(1024)] + [_const_spec(w.shape) for w in (wpa, wpb, wo)],
        out_specs=row(1024),
        out_shape=jax.ShapeDtypeStruct((T, D_MODEL), F32),
        compiler_params=_cparams("parallel"),
        name="merge",
    )(ha, ao, gs, x2, wpa, wpb, wo)


N_EXTRACT = PEER_TOPK + 1
ROUTE_HEADS_PER_ITER = 8


def _top_rows(s, n):
    rows = []
    for _ in range(n):
        m = jnp.max(s, axis=0, keepdims=True)
        rows.append(m)
        s = jnp.where(s == m, -jnp.inf, s)
    return rows


def _stack8(rows, t):
    ri = lax.broadcasted_iota(jnp.int32, (8, t), 0)
    out = jnp.full((8, t), -jnp.inf, F32)
    for k, r in enumerate(rows):
        out = jnp.where(ri == k, r, out)
    return out


def _route_head(h, q_scr, k1, k2, p1_ref, p2_ref, th_ref):
    qh = q_scr[h]
    t = qh.shape[0]
    s1 = _dot_nt(k1, qh[:, :PEER_HALF])
    s2 = _dot_nt(k2, qh[:, PEER_HALF:])
    v1 = _top_rows(s1, N_EXTRACT)
    v2 = _top_rows(s2, N_EXTRACT)
    v2a, v2b, v2c = _stack8(v2[0:8], t), _stack8(v2[8:16], t), _stack8(v2[16:17], t)
    v1b, v1c = _stack8(v1[8:16], t), _stack8(v1[16:17], t)
    cand = jnp.concatenate([v1[i] + v2a for i in range(8)] +
                           [v1[0] + v2b, v1[0] + v2c, v1b + v2[0], v1c + v2[0]], axis=0)
    best = _top_rows(cand, N_EXTRACT)
    z = jnp.zeros_like(best[0])
    for b in best[:PEER_TOPK]:
        z = z + jnp.exp(b - best[0])
    theta = 0.5 * (best[PEER_TOPK - 1] + best[PEER_TOPK])
    m1, m2 = v1[0], v2[0]
    c = RSQRT2 / z
    p1_ref[h] = jnp.exp(s1 - m1) * c
    p2_ref[h] = jnp.exp(s2 - m2)
    th_ref[pl.ds(h, 1), :] = jnp.exp(theta - m1 - m2) * c


def _route_kernel(x_ref, g_ref, wq_ref, k1_ref, k2_ref, xt_ref, p1_ref, p2_ref, th_ref, q_scr):
    xn = _rms(x_ref[...], g_ref[...])
    xt_ref[...] = xn.T.astype(BF16)
    q = _dot(xn.astype(BF16), wq_ref[...])
    for h in range(PEER_HEADS):
        q_scr[h] = q[:, h * PEER_DQ:(h + 1) * PEER_DQ].astype(BF16)
    k1, k2 = k1_ref[...], k2_ref[...]

    def head_group(i, carry):
        for k in range(ROUTE_HEADS_PER_ITER):
            _route_head(ROUTE_HEADS_PER_ITER * i + k, q_scr, k1, k2, p1_ref, p2_ref, th_ref)
        return carry

    lax.fori_loop(0, PEER_HEADS // ROUTE_HEADS_PER_ITER, head_group, 0)


def _route(x1, g, wq, k1, k2, tr):
    T = x1.shape[0]
    return pl.pallas_call(
        _route_kernel,
        grid=(T // tr,),
        in_specs=[pl.BlockSpec((tr, D_MODEL), lambda i: (i, 0)), _const_spec(g.shape), _const_spec(wq.shape),
                  _const_spec(k1.shape), _const_spec(k2.shape)],
        out_specs=[pl.BlockSpec((D_MODEL, tr), lambda i: (0, i)),
                   pl.BlockSpec((PEER_HEADS, PEER_NKEYS, tr), lambda i: (0, 0, i)),
                   pl.BlockSpec((PEER_HEADS, PEER_NKEYS, tr), lambda i: (0, 0, i)),
                   pl.BlockSpec((PEER_HEADS, tr), lambda i: (0, i))],
        out_shape=[jax.ShapeDtypeStruct((D_MODEL, T), BF16),
                   jax.ShapeDtypeStruct((PEER_HEADS, PEER_NKEYS, T), F32),
                   jax.ShapeDtypeStruct((PEER_HEADS, PEER_NKEYS, T), F32),
                   jax.ShapeDtypeStruct((PEER_HEADS, T), F32)],
        scratch_shapes=[pltpu.VMEM((PEER_HEADS, tr, PEER_DQ), BF16)],
        compiler_params=_cparams("parallel"),
        name="route",
    )(x1, g, wq, k1, k2)


E1_PER_STEP = 32
RSQRT2 = 0.7071067811865476


def _peer_kernel(xt_ref, u_ref, vt_ref, p1_ref, p2_ref, th_ref, x1_ref, o_ref, acc_scr, g_scr):
    s = pl.program_id(0)
    nblk = PEER_EXPERTS // (E1_PER_STEP * PEER_NKEYS)

    @pl.when(s == 0)
    def _():
        acc_scr[...] = jnp.zeros_like(acc_scr)
        g_scr[...] = jnp.zeros_like(g_scr)

    @pl.when((lax.rem(s, nblk) == 1) & (s > 1))
    def _():
        o_ref[...] = x1_ref[...] + acc_scr[...].T
        acc_scr[...] = jnp.zeros_like(acc_scr)

    slot = lax.rem(s, 2)
    acc_scr[...] += _dot(vt_ref[...], g_scr[1 - slot])
    xt = xt_ref[...]
    for e in range(E1_PER_STEP):
        sl = slice(e * PEER_NKEYS, (e + 1) * PEER_NKEYS)
        a = _dot(u_ref[sl, :], xt)
        act = a * (1.0 + lax.erf(a))
        w = jnp.zeros_like(a)
        for h in range(PEER_HEADS):
            p = p2_ref[h] * p1_ref[h, e:e + 1, :]
            w = w + jnp.where(p >= th_ref[h:h + 1, :], p, 0.0)
        g_scr[slot, sl, :] = (w * act).astype(BF16)


def _peer(xt, u, vt, p1, p2, th, x1, tb):
    T = x1.shape[0]
    eb = E1_PER_STEP * PEER_NKEYS
    nblk = PEER_EXPERTS // eb
    nt = T // tb
    cur = lambda s: jnp.minimum(s // nblk, nt - 1)
    done = lambda s: jnp.maximum(s - 2, 0) // nblk
    return pl.pallas_call(
        _peer_kernel,
        grid=(nt * nblk + 2,),
        in_specs=[pl.BlockSpec((D_MODEL, tb), lambda s: (0, cur(s))),
                  pl.BlockSpec((eb, D_MODEL), lambda s: (s % nblk, 0)),
                  pl.BlockSpec((D_MODEL, eb), lambda s: (0, (s + nblk - 1) % nblk)),
                  pl.BlockSpec((PEER_HEADS, E1_PER_STEP, tb), lambda s: (0, s % nblk, cur(s))),
                  pl.BlockSpec((PEER_HEADS, PEER_NKEYS, tb), lambda s: (0, 0, cur(s))),
                  pl.BlockSpec((PEER_HEADS, tb), lambda s: (0, cur(s))),
                  pl.BlockSpec((tb, D_MODEL), lambda s: (done(s), 0))],
        out_specs=pl.BlockSpec((tb, D_MODEL), lambda s: (done(s), 0)),
        out_shape=jax.ShapeDtypeStruct((T, D_MODEL), F32),
        scratch_shapes=[pltpu.VMEM((D_MODEL, tb), F32), pltpu.VMEM((2, eb, tb), BF16)],
        compiler_params=_cparams("arbitrary"),
        name="peer",
    )(xt, u, vt, p1, p2, th, x1)


def _prep_weights(p):
    w_in = p["w_in"]
    sizes = (A_HEADS * A_DQK, A_HEADS * A_DQK, A_HEADS * A_DV, A_HEADS * A_DV, 4 * A_HEADS,
             Q_LORA, KV_LORA, B_ROPE, 2 * D_MODEL)
    pts = np.cumsum((0,) + sizes)
    w_q, w_k, w_v, w_o, w_gate, w_cq, w_ckv, w_kr, w_gbr = (w_in[:, pts[i]:pts[i + 1]] for i in range(9))

    def pad_heads(w):
        w = w.reshape(D_MODEL, A_HEADS, A_DQK)
        return jnp.pad(w, ((0, 0), (0, 0), (0, LANE - A_DQK))).reshape(D_MODEL, A_HEADS * LANE)

    out = {}
    out["wa"] = jnp.concatenate([pad_heads(w_q) * (A_DQK ** -0.5), w_v, w_o], axis=1).astype(BF16)
    out["wc"] = jnp.concatenate([w_cq, w_ckv], axis=1).astype(BF16)
    wg4 = w_gate.reshape(D_MODEL, 4, A_HEADS)
    w_fal = jnp.stack([wg4[:, 1], wg4[:, 1], wg4[:, 3], wg4[:, 3]], axis=1).reshape(D_MODEL, 4 * A_HEADS)
    out["wkg"] = jnp.concatenate([w_kr, w_gate, w_fal], axis=1).astype(BF16)
    out["wg"] = w_gbr.astype(BF16)
    out["wkt"] = pad_heads(w_k).T.astype(BF16)
    out["wgt"] = w_gate.T.astype(BF16)
    bias = p["b_mgate"].astype(F32).reshape(4 * A_HEADS)
    b4 = bias.reshape(4, A_HEADS)
    b_fal = jnp.stack([b4[1], b4[1], b4[3], b4[3]]).reshape(4 * A_HEADS)
    out["bkg"] = jnp.concatenate([jnp.zeros((B_ROPE,), F32), bias, b_fal]).reshape(1, LANE)
    out["bgt"] = bias.reshape(4 * A_HEADS, 1)
    out["g_mix"] = p["norm_mix_g"].astype(F32).reshape(1, D_MODEL)
    out["g_mlstm"] = p["mlstm_norm_g"].astype(F32).reshape(1, A_HEADS * A_DV)
    out["gq"] = p["q_a_norm_g"].astype(F32).reshape(1, Q_LORA)
    out["gkv"] = p["kv_a_norm_g"].astype(F32).reshape(1, KV_LORA)
    wq = p["w_q_up"].reshape(Q_LORA, B_HEADS, B_DQK).transpose(1, 0, 2)
    out["wq"] = jnp.pad(wq, ((0, 0), (0, 0), (0, QK_PAD - B_DQK))).astype(BF16)
    wkv = p["w_kv_up"].reshape(KV_LORA, B_HEADS, B_NOPE + B_DV).transpose(1, 0, 2)
    out["wkbt"] = wkv[:, :, :B_NOPE].transpose(0, 2, 1).astype(BF16)
    out["wv"] = wkv[:, :, B_NOPE:].astype(BF16)
    out["nq"] = jnp.pad(p["qk_norm_q_g"].astype(F32), (0, QK_PAD - B_DQK)).reshape(1, QK_PAD)
    nk = p["qk_norm_k_g"].astype(F32)
    out["nk_col"] = jnp.pad(nk, (0, QK_PAD - B_DQK)).reshape(QK_PAD, 1)
    out["kb"] = (1.01 * B_DQK ** 0.5 * jnp.max(jnp.abs(nk))).reshape(1, 1)
    out["score_bound"] = 1.01 * B_DQK ** 0.5 * LOG2E * jnp.max(jnp.abs(nk)) * jnp.max(jnp.abs(p["qk_norm_q_g"]))
    out["wpa"] = p["w_proj_a"].astype(BF16)
    out["wpb"] = p["w_proj_b"].astype(BF16)
    out["wo"] = p["w_out"].astype(BF16)
    out["g_ffn"] = p["norm_ffn_g"].astype(F32).reshape(1, D_MODEL)
    out["wpq"] = p["w_peer_q"].astype(BF16)
    out["k1"] = p["peer_keys1"].astype(BF16)
    out["k2"] = p["peer_keys2"].astype(BF16)
    out["u"] = (p["peer_u"] * RSQRT2).astype(BF16)
    out["vt"] = p["peer_v"].T.astype(BF16)
    return out


def _rope_tables(S):
    pos = jnp.arange(S, dtype=F32)
    inv = ROPE_THETA ** (-jnp.arange(0, B_ROPE, 2, dtype=F32) / B_ROPE)
    ang = pos[:, None] * inv[None, :]
    ang = jnp.concatenate([ang, ang], axis=-1)
    sign = jnp.where(jnp.arange(B_ROPE) < B_ROPE // 2, -1.0, 1.0).astype(F32)
    pad = ((0, 0), (0, LANE - B_ROPE))
    cos, sin = jnp.pad(jnp.cos(ang), pad), jnp.pad(jnp.sin(ang) * sign, pad)
    return cos, sin, cos.T, sin.T


def _pick(n, pref):
    t = min(n, pref)
    assert n % t == 0, (n, t)
    return t


def _layer(x, w):
    B, S, _ = x.shape
    T = B * S
    x2 = x.reshape(T, D_MODEL)
    qa, va, osig, cqkv, kg, gs, kt, gt, kb, bt = _inproj(
        x2, w["g_mix"], w["wa"], w["wc"], w["wkg"], w["wg"], w["wkt"], w["wgt"], w["bkg"], w["bgt"],
        _pick(T, 256))
    hf = _mlstm(qa, kt, va, kb, gt, bt, B, S, False)
    ha = _mlstm(qa, kt, va, kb, gt, bt, B, S, True, hf, osig, w["g_mlstm"])
    cos, sin, cost, sint = _rope_tables(S)
    ts = _pick(S, 256)
    nkt = jnp.broadcast_to(w["nk_col"], (QK_PAD, ts))
    q, kbt, v = _mlaprep(cqkv, kg, cos, sin, cost, sint, w["gq"], w["gkv"], w["wq"], w["wkbt"], w["wv"], w["nq"],
                         nkt, w["kb"], B, S, ts)
    tq, tk = _pick(S, 512), _pick(S, 512)
    ao = lax.cond(w["score_bound"] <= 50.0,
                  lambda *a: _attention(*a, tq, tk, True),
                  lambda *a: _attention(*a, tq, tk, False),
                  q, kbt, v).reshape(T, B_HEADS * B_DV)
    x1 = _merge(ha, ao, gs, x2, w["wpa"], w["wpb"], w["wo"], _pick(T, 512))
    xt, p1, p2, th = _route(x1, w["g_ffn"], w["wpq"], w["k1"], w["k2"], _pick(T, 256))
    y = _peer(xt, w["u"], w["vt"], p1, p2, th, x1, _pick(T, 256))
    return y.reshape(B, S, D_MODEL)


def kernel(x_prompt, x_sample, norm_mix_g, w_in, b_mgate, mlstm_norm_g, q_a_norm_g, w_q_up, kv_a_norm_g, w_kv_up, qk_norm_q_g, qk_norm_k_g, w_proj_a, w_proj_b, w_out, norm_ffn_g, w_peer_q, peer_keys1, peer_keys2, peer_u, peer_v):
    params = dict(norm_mix_g=norm_mix_g, w_in=w_in, b_mgate=b_mgate, mlstm_norm_g=mlstm_norm_g,
                  q_a_norm_g=q_a_norm_g, w_q_up=w_q_up, kv_a_norm_g=kv_a_norm_g, w_kv_up=w_kv_up,
                  qk_norm_q_g=qk_norm_q_g, qk_norm_k_g=qk_norm_k_g, w_proj_a=w_proj_a, w_proj_b=w_proj_b,
                  w_out=w_out, norm_ffn_g=norm_ffn_g, w_peer_q=w_peer_q, peer_keys1=peer_keys1,
                  peer_keys2=peer_keys2, peer_u=peer_u, peer_v=peer_v)
    depth = w_in.shape[0]
    layers = [_prep_weights({k: v[l] for k, v in params.items()}) for l in range(depth)]

    def run(x):
        for w in layers:
            x = _layer(x, w)
        return x

    return (run(x_prompt), run(x_sample))
```

```python
import functools

import jax
import jax.numpy as jnp
import numpy as np
from jax import lax
from jax.experimental import pallas as pl
from jax.experimental.pallas import tpu as pltpu

F32 = jnp.float32
BF16 = jnp.bfloat16

D_MODEL = 1024
A_HEADS, A_DQK, A_DV, A_CHUNK = 8, 64, 128, 128
B_HEADS, B_NOPE, B_ROPE, B_DV = 8, 128, 64, 128
B_DQK = B_NOPE + B_ROPE
Q_LORA, KV_LORA = 384, 256
ROPE_THETA = 10000.0
PEER_HEADS, PEER_NKEYS, PEER_DQ, PEER_TOPK = 8, 128, 256, 16
PEER_HALF = PEER_DQ // 2
PEER_EXPERTS = PEER_NKEYS * PEER_NKEYS
EPS = 1e-6

LANE = 128
QK_PAD = 2 * LANE
VMEM_LIMIT = 52 * 1024 * 1024


def _cparams(*sem):
    return pltpu.CompilerParams(dimension_semantics=sem, vmem_limit_bytes=VMEM_LIMIT)


def _dot(a, b):
    return jnp.dot(a, b, preferred_element_type=F32)


def _dot_nt(a, b):
    return lax.dot_general(a, b, (((1,), (1,)), ((), ())), preferred_element_type=F32)


def _rms(x, g):
    return x * lax.rsqrt(jnp.mean(x * x, axis=-1, keepdims=True) + EPS) * g


def _log_sigmoid(x):
    return jnp.minimum(x, 0.0) - jnp.log(1.0 + jnp.exp(-jnp.abs(x)))


def _sigmoid(x):
    return 1.0 / (1.0 + jnp.exp(-x))


def _const_spec(shape):
    nd = len(shape)
    return pl.BlockSpec(shape, lambda *_: (0,) * nd)


def _split3(x):
    hi = x.astype(BF16)
    r = x - hi.astype(F32)
    mid = r.astype(BF16)
    lo = (r - mid.astype(F32)).astype(BF16)
    return hi, mid, lo


def _dot3(a_f32, b_bf16):
    hi, mid, lo = _split3(a_f32)
    return _dot(hi, b_bf16) + _dot(mid, b_bf16) + _dot(lo, b_bf16)


def _dot3r(a_bf16, b_f32):
    hi, mid, lo = _split3(b_f32)
    return _dot(a_bf16, hi) + _dot(a_bf16, mid) + _dot(a_bf16, lo)


def _cummax_rows(y, reverse, period):
    n = y.shape[0]
    row = lax.broadcasted_iota(jnp.int32, y.shape, 0) % period
    k = 1
    while k < period:
        if reverse:
            sh = jnp.where(row < period - k, pltpu.roll(y, n - k, 0), -jnp.inf)
        else:
            sh = jnp.where(row >= k, pltpu.roll(y, k, 0), -jnp.inf)
        y = jnp.maximum(y, sh)
        k *= 2
    return y


def _inproj_kernel(x_ref, g_ref, wa_ref, wc_ref, wkg_ref, wg_ref, wkt_ref, wgt_ref, bkg_ref, bgt_ref,
                   qa_ref, va_ref, os_ref, cqkv_ref, kg_ref, gs_ref, kt_ref, gt_ref, kb_ref, bt_ref):
    h = _rms(x_ref[...], g_ref[...]).astype(BF16)
    a = _dot(h, wa_ref[...])
    qa_ref[...] = a[:, :1024].astype(BF16)
    va_ref[...] = a[:, 1024:2048].astype(BF16)
    os_ref[...] = _sigmoid(a[:, 2048:]).astype(BF16)
    cqkv_ref[...] = _dot(h, wc_ref[...])
    kg = _dot(h, wkg_ref[...]) + bkg_ref[...]
    lane = lax.broadcasted_iota(jnp.int32, kg.shape, 1)
    is_f = ((lane >= 72) & (lane < 80)) | (lane >= 88)
    kg = jnp.where(is_f, _log_sigmoid(kg), kg)
    kg_ref[...] = kg
    gs_ref[...] = _sigmoid(_dot(h, wg_ref[...])).astype(BF16)
    kt_ref[...] = _dot_nt(wkt_ref[...], h).astype(BF16)
    gt = _dot_nt(wgt_ref[...], h) + bgt_ref[...]
    row = lax.broadcasted_iota(jnp.int32, gt.shape, 0)
    is_fr = ((row >= 8) & (row < 16)) | (row >= 24)
    gt = jnp.where(is_fr, _log_sigmoid(gt), gt)
    gt_ref[...] = gt
    tm = kg.shape[0]
    r = lax.broadcasted_iota(jnp.int32, (tm, tm), 0)
    c = lax.broadcasted_iota(jnp.int32, (tm, tm), 1)
    same = (r // A_CHUNK) == (c // A_CHUNK)
    lower = jnp.where(same & (c <= r), 1.0, 0.0).astype(BF16)
    upper = jnp.where(same & (c >= r), 1.0, 0.0).astype(BF16)
    g, gal = kg[:, 64:96], kg[:, 96:128]
    fwd_l = lax.broadcasted_iota(jnp.int32, g.shape, 1) < 2 * A_HEADS
    b_al = jnp.where(fwd_l, _dot3r(lower, gal), _dot3r(upper, gal))
    y = jnp.concatenate([g - b_al, jnp.zeros((tm, LANE - 4 * A_HEADS), F32)], axis=1)
    cm = jnp.where(fwd_l, _cummax_rows(y, False, A_CHUNK)[:, :4 * A_HEADS],
                   _cummax_rows(y, True, A_CHUNK)[:, :4 * A_HEADS])
    kb_ref[...] = jnp.concatenate([b_al, cm, jnp.zeros((tm, LANE - 8 * A_HEADS), F32)], axis=1)
    fwd_r = lax.broadcasted_iota(jnp.int32, gt.shape, 0) < 2 * A_HEADS
    bt_ref[...] = jnp.where(fwd_r, _dot3(gt, upper), _dot3(gt, lower))


def _inproj(x2, g, wa, wc, wkg, wg, wkt, wgt, bkg, bgt, tm):
    T = x2.shape[0]
    row = lambda w: pl.BlockSpec((tm, w), lambda i: (i, 0))
    col = lambda r: pl.BlockSpec((r, tm), lambda i: (0, i))
    return pl.pallas_call(
        _inproj_kernel,
        grid=(T // tm,),
        in_specs=[row(D_MODEL), _const_spec(g.shape), _const_spec(wa.shape), _const_spec(wc.shape),
                  _const_spec(wkg.shape), _const_spec(wg.shape), _const_spec(wkt.shape),
                  _const_spec(wgt.shape), _const_spec(bkg.shape), _const_spec(bgt.shape)],
        out_specs=[row(1024), row(1024), row(1024), row(640), row(128), row(2048), col(1024), col(32), row(128),
                   col(32)],
        out_shape=[jax.ShapeDtypeStruct((T, 1024), BF16), jax.ShapeDtypeStruct((T, 1024), BF16),
                   jax.ShapeDtypeStruct((T, 1024), BF16), jax.ShapeDtypeStruct((T, 640), F32),
                   jax.ShapeDtypeStruct((T, 128), F32), jax.ShapeDtypeStruct((T, 2048), BF16),
                   jax.ShapeDtypeStruct((1024, T), BF16), jax.ShapeDtypeStruct((32, T), F32),
                   jax.ShapeDtypeStruct((T, 128), F32), jax.ShapeDtypeStruct((32, T), F32)],
        compiler_params=_cparams("parallel"),
        name="inproj",
    )(x2, g, wa, wc, wkg, wg, wkt, wgt, bkg, bgt)


def _mlstm_kernel(*refs, reverse):
    if reverse:
        q_ref, kt_ref, v_ref, kb_ref, gt_ref, bt_ref, hf_ref, os_ref, ng_ref, out_ref, cn_scr, m_scr = refs
    else:
        q_ref, kt_ref, v_ref, kb_ref, gt_ref, bt_ref, out_ref, cn_scr, m_scr = refs
    L = A_CHUNK

    @pl.when(pl.program_id(1) == 0)
    def _():
        cn_scr[...] = jnp.zeros_like(cn_scr)
        m_scr[...] = jnp.zeros_like(m_scr)

    row = lax.broadcasted_iota(jnp.int32, (L, L), 0)
    col = lax.broadcasted_iota(jnp.int32, (L, L), 1)
    mask = (col >= row) if reverse else (col <= row)
    gt = gt_ref[...]
    b_rows = bt_ref[...]
    bc3 = jnp.concatenate(_split3(kb_ref[:, :8 * A_HEADS]), axis=1)
    off = 16 if reverse else 0
    ones = jnp.ones((L, LANE), BF16)
    sel_r = lax.broadcasted_iota(jnp.int32, (6 * 4 * A_HEADS, 2 * LANE), 0)
    sel_c = lax.broadcasted_iota(jnp.int32, (6 * 4 * A_HEADS, 2 * LANE), 1)
    sel_r = sel_r % (8 * A_HEADS) - jnp.where(sel_c >= LANE, 4 * A_HEADS, 0)
    last = 0 if reverse else L - 1

    H = range(A_HEADS)
    sls = [slice(h * LANE, (h + 1) * LANE) for h in H]
    bc = [_dot(bc3, jnp.where(sel_r == off + h, 1.0, 0.0).astype(BF16)) for h in H]
    b_c = [x[:, :LANE] for x in bc]
    cm_c = [x[:, LANE:] for x in bc]
    qk = [_dot(q_ref[:, sls[h]], kt_ref[sls[h], :]) for h in H]
    cns = [cn_scr[h] for h in H]
    qc = [_dot(q_ref[:, sls[h]], cns[h].astype(BF16)) for h in H]
    ms = [m_scr[h] for h in H]
    i_r = [gt[off + h:off + h + 1, :] for h in H]
    b_r = [b_rows[off + 8 + h:off + 9 + h, :] for h in H]
    m_t = [b_c[h] + jnp.maximum(cm_c[h], ms[h]) for h in H]
    s = [(qk[h] * jnp.where(mask, jnp.exp(b_c[h] - b_r[h] + i_r[h] - m_t[h]), 0.0)).astype(BF16) for h in H]
    vext = [jnp.concatenate([v_ref[:, sls[h]], ones], axis=1) for h in H]
    sv = [_dot(s[h], vext[h]) for h in H]
    b_last = [b_c[h][last:last + 1] for h in H]
    m_new = [b_last[h] + jnp.maximum(ms[h], cm_c[h][last:last + 1]) for h in H]
    kts = [(kt_ref[sls[h], :].astype(F32) * jnp.exp(b_last[h] - b_r[h] + i_r[h] - m_new[h])).astype(BF16) for h in H]
    kv = [_dot(kts[h], vext[h]) for h in H]
    for h in H:
        decay = jnp.exp(b_last[h] + ms[h] - m_new[h])
        cn_scr[h] = jnp.concatenate([decay, decay], axis=1) * cns[h] + kv[h]
        m_scr[h] = m_new[h]
        inter_w = jnp.exp(b_c[h] + ms[h] - m_t[h])
        tot = sv[h] + jnp.concatenate([inter_w, inter_w], axis=1) * qc[h]
        hh = tot[:, :LANE] / jnp.maximum(jnp.abs(tot[:, LANE:]), jnp.exp(-m_t[h]))
        if reverse:
            t = hh + hf_ref[:, sls[h]]
            y = t * lax.rsqrt(jnp.mean(t * t, axis=-1, keepdims=True) + EPS) * ng_ref[:, sls[h]]
            out_ref[:, sls[h]] = (y * os_ref[:, sls[h]].astype(F32)).astype(BF16)
        else:
            out_ref[:, sls[h]] = hh


def _mlstm(qa, kt, va, kb, gt, bt, B, S, reverse, hf=None, osig=None, ng=None):
    L = A_CHUNK
    nc = S // L
    T = B * S
    if reverse:
        blk = lambda b, c: b * nc + (nc - 1 - c)
    else:
        blk = lambda b, c: b * nc + c
    row = lambda w: pl.BlockSpec((L, w), lambda b, c: (blk(b, c), 0))
    col = lambda r: pl.BlockSpec((r, L), lambda b, c: (0, blk(b, c)))
    in_specs = [row(1024), col(1024), row(1024), row(128), col(32), col(32)]
    args = [qa, kt, va, kb, gt, bt]
    if reverse:
        in_specs += [row(1024), row(1024), _const_spec(ng.shape)]
        args += [hf, osig, ng]
    return pl.pallas_call(
        functools.partial(_mlstm_kernel, reverse=reverse),
        grid=(B, nc),
        in_specs=in_specs,
        out_specs=row(1024),
        out_shape=jax.ShapeDtypeStruct((T, 1024), BF16 if reverse else F32),
        scratch_shapes=[pltpu.VMEM((A_HEADS, LANE, 2 * LANE), F32), pltpu.VMEM((A_HEADS, 1, LANE), F32)],
        compiler_params=_cparams("parallel", "arbitrary"),
        name="mlstm_bwd" if reverse else "mlstm_fwd",
    )(*args)


def _rope(x, cos, sin_signed, axis):
    idx = lax.broadcasted_iota(jnp.int32, x.shape, axis)
    half = B_ROPE // 2
    rot = jnp.where(idx < half, pltpu.roll(x, LANE - half, axis), pltpu.roll(x, half, axis))
    return x * cos + rot * sin_signed


BOUND_COL = B_DQK - LANE
LOG2E = 1.4426950408889634


def _mlaprep_kernel(cqkv_ref, kg_ref, cos_ref, sin_ref, cost_ref, sint_ref, gq_ref, gkv_ref, wq_ref, wkt_ref,
                    wv_ref, nq_ref, nkt_ref, sb_ref, q_ref, kt_ref, v_ref):
    ts = cqkv_ref.shape[0]
    cq = _rms(cqkv_ref[:, :Q_LORA], gq_ref[...]).astype(BF16)
    ckv = _rms(cqkv_ref[:, Q_LORA:], gkv_ref[...]).astype(BF16)
    kgt = kg_ref[...].T
    rowi = lax.broadcasted_iota(jnp.int32, kgt.shape, 0)
    krt = jnp.where(rowi < B_ROPE, kgt, 0.0)
    ss_r = jnp.sum(krt * krt, axis=0, keepdims=True)
    nkt = nkt_ref[...]
    krt_rot = _rope(krt * nkt[LANE:], cost_ref[...], sint_ref[...], 0)
    one_row = jnp.where(rowi == BOUND_COL, 1.0, 0.0)
    nq = nq_ref[...]
    cos, sin = cos_ref[...], sin_ref[...]
    lane = lax.broadcasted_iota(jnp.int32, (ts, LANE), 1)
    scale = B_DQK ** -0.5 * LOG2E
    bound = sb_ref[...]
    ones = jnp.ones((ts, B_DV), BF16)
    for h in range(B_HEADS):
        q = _dot(cq, wq_ref[h])
        rq = lax.rsqrt(jnp.sum(q * q, axis=-1, keepdims=True) * (1.0 / B_DQK) + EPS) * scale
        qn = q * nq
        q_ref[0, h, :, :LANE] = (qn[:, :LANE] * rq).astype(BF16)
        q_rot = _rope(qn[:, LANE:], cos, sin, 1) * rq
        q_ref[0, h, :, LANE:] = jnp.where(lane == BOUND_COL, -bound, q_rot).astype(BF16)
        knt = _dot_nt(wkt_ref[h], ckv)
        rk = lax.rsqrt((jnp.sum(knt * knt, axis=0, keepdims=True) + ss_r) * (1.0 / B_DQK) + EPS)
        kt_ref[0, h, :LANE, :] = (knt * nkt[:LANE] * rk).astype(BF16)
        kt_ref[0, h, LANE:, :] = (krt_rot * rk + one_row).astype(BF16)
        v_ref[0, h, :, :B_DV] = _dot(ckv, wv_ref[h]).astype(BF16)
        v_ref[0, h, :, B_DV:] = ones


def _mlaprep(cqkv, kg, cos, sin, cost, sint, gq, gkv, wq, wkt, wv, nq, nkt, sb, B, S, ts):
    ns = S // ts
    row = lambda w: pl.BlockSpec((ts, w), lambda b, i: (b * ns + i, 0))
    pos = pl.BlockSpec((ts, LANE), lambda b, i: (i, 0))
    post = pl.BlockSpec((LANE, ts), lambda b, i: (0, i))
    consts = (gq, gkv, wq, wkt, wv, nq, nkt, sb)
    return pl.pallas_call(
        _mlaprep_kernel,
        grid=(B, ns),
        in_specs=[row(640), row(128), pos, pos, post, post] + [_const_spec(a.shape) for a in consts],
        out_specs=[pl.BlockSpec((1, B_HEADS, ts, QK_PAD), lambda b, i: (b, 0, i, 0)),
                   pl.BlockSpec((1, B_HEADS, QK_PAD, ts), lambda b, i: (b, 0, 0, i)),
                   pl.BlockSpec((1, B_HEADS, ts, 2 * B_DV), lambda b, i: (b, 0, i, 0))],
        out_shape=[jax.ShapeDtypeStruct((B, B_HEADS, S, QK_PAD), BF16),
                   jax.ShapeDtypeStruct((B, B_HEADS, QK_PAD, S), BF16),
                   jax.ShapeDtypeStruct((B, B_HEADS, S, 2 * B_DV), BF16)],
        compiler_params=_cparams("parallel", "parallel"),
        name="mlaprep",
    )(cqkv, kg, cos, sin, cost, sint, *consts)


def _attn_kernel(q_ref, kt_ref, v_ref, o_ref, *, tk, nk):
    q = q_ref[0, 0]
    tq = q.shape[0]

    def body(j, carry):
        m, acc = carry
        start = pl.multiple_of(j * tk, tk)
        kt = kt_ref[0, 0, :, pl.ds(start, tk)]
        v = v_ref[0, 0, pl.ds(start, tk), :]
        s = _dot(q, kt)
        m_new = jnp.maximum(m, jnp.max(s, axis=1, keepdims=True))
        p = jnp.exp2(s - m_new)
        acc = jnp.exp2(m - m_new) * acc + _dot(p.astype(BF16), v)
        return m_new, acc

    init = (jnp.full((tq, 1), -jnp.inf, F32), jnp.zeros((tq, 2 * B_DV), F32))
    _, acc = lax.fori_loop(0, nk, body, init)
    o_ref[0] = (acc[:, :B_DV] / acc[:, B_DV:]).astype(BF16)


def _attn_fast_kernel(q_ref, kt_ref, v_ref, o_ref, *, tk, nk, unroll):
    q = q_ref[0, 0]
    tq = q.shape[0]

    def probs(j):
        kt = kt_ref[0, 0, :, pl.ds(pl.multiple_of(j * tk, tk), tk)]
        return jnp.exp2(_dot(q, kt)).astype(BF16)

    def weighted(j, p, acc):
        v = v_ref[0, 0, pl.ds(pl.multiple_of(j * tk, tk), tk), :]
        return acc + _dot(p, v)

    def body(j, carry):
        p, acc = carry
        return probs(j), weighted(j - 1, p, acc)

    p, acc = lax.fori_loop(1, nk, body, (probs(0), jnp.zeros((tq, 2 * B_DV), F32)), unroll=unroll)
    acc = weighted(nk - 1, p, acc)
    o_ref[0] = (acc[:, :B_DV] / acc[:, B_DV:]).astype(BF16)


def _attention(q, kt, v, tq, tk, fast):
    B, H, S, _ = q.shape
    nk = S // tk
    if fast:
        body, name = functools.partial(_attn_fast_kernel, tk=tk, nk=nk, unroll=max(1, min(8, nk - 1))), "attention_fast"
    else:
        body, name = functools.partial(_attn_kernel, tk=tk, nk=nk), "attention"
    return pl.pallas_call(
        body,
        grid=(B, H, S // tq),
        in_specs=[pl.BlockSpec((1, 1, tq, QK_PAD), lambda b, h, i: (b, h, i, 0)),
                  pl.BlockSpec((1, 1, QK_PAD, S), lambda b, h, i: (b, h, 0, 0)),
                  pl.BlockSpec((1, 1, S, 2 * B_DV), lambda b, h, i: (b, h, 0, 0))],
        out_specs=pl.BlockSpec((1, tq, B_DV), lambda b, h, i: (b, i, h)),
        out_shape=jax.ShapeDtypeStruct((B, S, H * B_DV), BF16),
        compiler_params=_cparams("parallel", "parallel", "arbitrary"),
        name=name,
    )(q, kt, v)


def _merge_kernel(ha_ref, ao_ref, gs_ref, x_ref, wpa_ref, wpb_ref, wo_ref, o_ref):
    ya = _dot(ha_ref[...], wpa_ref[...])
    yb = _dot(ao_ref[...], wpb_ref[...])
    merged = gs_ref[:, :D_MODEL].astype(F32) * ya + gs_ref[:, D_MODEL:].astype(F32) * yb
    o_ref[...] = x_ref[...] + _dot(merged.astype(BF16), wo_ref[...])


def _merge(ha, ao, gs, x2, wpa, wpb, wo, tm):
    T = x2.shape[0]
    row = lambda w: pl.BlockSpec((tm, w), lambda i: (i, 0))
    return pl.pallas_call(
        _merge_kernel,
        grid=(T // tm,),
        in_specs=[row(1024), row(1024), row(2048), row(1024)] + [_const_spec(w.shape) for w in (wpa, wpb, wo)],
        out_specs=row(1024),
        out_shape=jax.ShapeDtypeStruct((T, D_MODEL), F32),
        compiler_params=_cparams("parallel"),
        name="merge",
    )(ha, ao, gs, x2, wpa, wpb, wo)


N_EXTRACT = PEER_TOPK + 1
ROUTE_HEADS_PER_ITER = 8


def _top_rows(s, n):
    rows = []
    for _ in range(n):
        m = jnp.max(s, axis=0, keepdims=True)
        rows.append(m)
        s = jnp.where(s == m, -jnp.inf, s)
    return rows


def _stack8(rows, t):
    ri = lax.broadcasted_iota(jnp.int32, (8, t), 0)
    out = jnp.full((8, t), -jnp.inf, F32)
    for k, r in enumerate(rows):
        out = jnp.where(ri == k, r, out)
    return out


def _route_head(h, q_scr, k1, k2, p1_ref, p2_ref, th_ref):
    qh = q_scr[h]
    t = qh.shape[0]
    s1 = _dot_nt(k1, qh[:, :PEER_HALF])
    s2 = _dot_nt(k2, qh[:, PEER_HALF:])
    v1 = _top_rows(s1, N_EXTRACT)
    v2 = _top_rows(s2, N_EXTRACT)
    v2a, v2b, v2c = _stack8(v2[0:8], t), _stack8(v2[8:16], t), _stack8(v2[16:17], t)
    v1b, v1c = _stack8(v1[8:16], t), _stack8(v1[16:17], t)
    cand = jnp.concatenate([v1[i] + v2a for i in range(8)] +
                           [v1[0] + v2b, v1[0] + v2c, v1b + v2[0], v1c + v2[0]], axis=0)
    best = _top_rows(cand, N_EXTRACT)
    z = jnp.zeros_like(best[0])
    for b in best[:PEER_TOPK]:
        z = z + jnp.exp(b - best[0])
    theta = 0.5 * (best[PEER_TOPK - 1] + best[PEER_TOPK])
    m1, m2 = v1[0], v2[0]
    c = RSQRT2 / z
    p1_ref[h] = jnp.exp(s1 - m1) * c
    p2_ref[h] = jnp.exp(s2 - m2)
    th_ref[pl.ds(h, 1), :] = jnp.exp(theta - m1 - m2) * c


def _route_kernel(x_ref, g_ref, wq_ref, k1_ref, k2_ref, xt_ref, p1_ref, p2_ref, th_ref, q_scr):
    xn = _rms(x_ref[...], g_ref[...])
    xt_ref[...] = xn.T.astype(BF16)
    q = _dot(xn.astype(BF16), wq_ref[...])
    for h in range(PEER_HEADS):
        q_scr[h] = q[:, h * PEER_DQ:(h + 1) * PEER_DQ].astype(BF16)
    k1, k2 = k1_ref[...], k2_ref[...]

    def head_group(i, carry):
        for k in range(ROUTE_HEADS_PER_ITER):
            _route_head(ROUTE_HEADS_PER_ITER * i + k, q_scr, k1, k2, p1_ref, p2_ref, th_ref)
        return carry

    lax.fori_loop(0, PEER_HEADS // ROUTE_HEADS_PER_ITER, head_group, 0)


def _route(x1, g, wq, k1, k2, tr):
    T = x1.shape[0]
    return pl.pallas_call(
        _route_kernel,
        grid=(T // tr,),
        in_specs=[pl.BlockSpec((tr, D_MODEL), lambda i: (i, 0)), _const_spec(g.shape), _const_spec(wq.shape),
                  _const_spec(k1.shape), _const_spec(k2.shape)],
        out_specs=[pl.BlockSpec((D_MODEL, tr), lambda i: (0, i)),
                   pl.BlockSpec((PEER_HEADS, PEER_NKEYS, tr), lambda i: (0, 0, i)),
                   pl.BlockSpec((PEER_HEADS, PEER_NKEYS, tr), lambda i: (0, 0, i)),
                   pl.BlockSpec((PEER_HEADS, tr), lambda i: (0, i))],
        out_shape=[jax.ShapeDtypeStruct((D_MODEL, T), BF16),
                   jax.ShapeDtypeStruct((PEER_HEADS, PEER_NKEYS, T), F32),
                   jax.ShapeDtypeStruct((PEER_HEADS, PEER_NKEYS, T), F32),
                   jax.ShapeDtypeStruct((PEER_HEADS, T), F32)],
        scratch_shapes=[pltpu.VMEM((PEER_HEADS, tr, PEER_DQ), BF16)],
        compiler_params=_cparams("parallel"),
        name="route",
    )(x1, g, wq, k1, k2)


E1_PER_STEP = 32
RSQRT2 = 0.7071067811865476


def _peer_kernel(xt_ref, u_ref, vt_ref, p1_ref, p2_ref, th_ref, x1_ref, o_ref, acc_scr, g_scr):
    s = pl.program_id(0)
    nblk = PEER_EXPERTS // (E1_PER_STEP * PEER_NKEYS)

    @pl.when(s == 0)
    def _():
        acc_scr[...] = jnp.zeros_like(acc_scr)
        g_scr[...] = jnp.zeros_like(g_scr)

    @pl.when((lax.rem(s, nblk) == 1) & (s > 1))
    def _():
        o_ref[...] = x1_ref[...] + acc_scr[...].T
        acc_scr[...] = jnp.zeros_like(acc_scr)

    slot = lax.rem(s, 2)
    acc_scr[...] += _dot(vt_ref[...], g_scr[1 - slot])
    xt = xt_ref[...]
    for e in range(E1_PER_STEP):
        sl = slice(e * PEER_NKEYS, (e + 1) * PEER_NKEYS)
        a = _dot(u_ref[sl, :], xt)
        act = a * (1.0 + lax.erf(a))
        w = jnp.zeros_like(a)
        for h in range(PEER_HEADS):
            p = p2_ref[h] * p1_ref[h, e:e + 1, :]
            w = w + jnp.where(p >= th_ref[h:h + 1, :], p, 0.0)
        g_scr[slot, sl, :] = (w * act).astype(BF16)


def _peer(xt, u, vt, p1, p2, th, x1, tb):
    T = x1.shape[0]
    eb = E1_PER_STEP * PEER_NKEYS
    nblk = PEER_EXPERTS // eb
    nt = T // tb
    cur = lambda s: jnp.minimum(s // nblk, nt - 1)
    done = lambda s: jnp.maximum(s - 2, 0) // nblk
    return pl.pallas_call(
        _peer_kernel,
        grid=(nt * nblk + 2,),
        in_specs=[pl.BlockSpec((D_MODEL, tb), lambda s: (0, cur(s))),
                  pl.BlockSpec((eb, D_MODEL), lambda s: (s % nblk, 0)),
                  pl.BlockSpec((D_MODEL, eb), lambda s: (0, (s + nblk - 1) % nblk)),
                  pl.BlockSpec((PEER_HEADS, E1_PER_STEP, tb), lambda s: (0, s % nblk, cur(s))),
                  pl.BlockSpec((PEER_HEADS, PEER_NKEYS, tb), lambda s: (0, 0, cur(s))),
                  pl.BlockSpec((PEER_HEADS, tb), lambda s: (0, cur(s))),
                  pl.BlockSpec((tb, D_MODEL), lambda s: (done(s), 0))],
        out_specs=pl.BlockSpec((tb, D_MODEL), lambda s: (done(s), 0)),
        out_shape=jax.ShapeDtypeStruct((T, D_MODEL), F32),
        scratch_shapes=[pltpu.VMEM((D_MODEL, tb), F32), pltpu.VMEM((2, eb, tb), BF16)],
        compiler_params=_cparams("arbitrary"),
        name="peer",
    )(xt, u, vt, p1, p2, th, x1)


def _prep_weights(p):
    w_in = p["w_in"]
    sizes = (A_HEADS * A_DQK, A_HEADS * A_DQK, A_HEADS * A_DV, A_HEADS * A_DV, 4 * A_HEADS,
             Q_LORA, KV_LORA, B_ROPE, 2 * D_MODEL)
    pts = np.cumsum((0,) + sizes)
    w_q, w_k, w_v, w_o, w_gate, w_cq, w_ckv, w_kr, w_gbr = (w_in[:, pts[i]:pts[i + 1]] for i in range(9))

    def pad_heads(w):
        w = w.reshape(D_MODEL, A_HEADS, A_DQK)
        return jnp.pad(w, ((0, 0), (0, 0), (0, LANE - A_DQK))).reshape(D_MODEL, A_HEADS * LANE)

    out = {}
    out["wa"] = jnp.concatenate([pad_heads(w_q) * (A_DQK ** -0.5), w_v, w_o], axis=1).astype(BF16)
    out["wc"] = jnp.concatenate([w_cq, w_ckv], axis=1).astype(BF16)
    wg4 = w_gate.reshape(D_MODEL, 4, A_HEADS)
    w_fal = jnp.stack([wg4[:, 1], wg4[:, 1], wg4[:, 3], wg4[:, 3]], axis=1).reshape(D_MODEL, 4 * A_HEADS)
    out["wkg"] = jnp.concatenate([w_kr, w_gate, w_fal], axis=1).astype(BF16)
    out["wg"] = w_gbr.astype(BF16)
    out["wkt"] = pad_heads(w_k).T.astype(BF16)
    out["wgt"] = w_gate.T.astype(BF16)
    bias = p["b_mgate"].astype(F32).reshape(4 * A_HEADS)
    b4 = bias.reshape(4, A_HEADS)
    b_fal = jnp.stack([b4[1], b4[1], b4[3], b4[3]]).reshape(4 * A_HEADS)
    out["bkg"] = jnp.concatenate([jnp.zeros((B_ROPE,), F32), bias, b_fal]).reshape(1, LANE)
    out["bgt"] = bias.reshape(4 * A_HEADS, 1)
    out["g_mix"] = p["norm_mix_g"].astype(F32).reshape(1, D_MODEL)
    out["g_mlstm"] = p["mlstm_norm_g"].astype(F32).reshape(1, A_HEADS * A_DV)
    out["gq"] = p["q_a_norm_g"].astype(F32).reshape(1, Q_LORA)
    out["gkv"] = p["kv_a_norm_g"].astype(F32).reshape(1, KV_LORA)
    wq = p["w_q_up"].reshape(Q_LORA, B_HEADS, B_DQK).transpose(1, 0, 2)
    out["wq"] = jnp.pad(wq, ((0, 0), (0, 0), (0, QK_PAD - B_DQK))).astype(BF16)
    wkv = p["w_kv_up"].reshape(KV_LORA, B_HEADS, B_NOPE + B_DV).transpose(1, 0, 2)
    out["wkbt"] = wkv[:, :, :B_NOPE].transpose(0, 2, 1).astype(BF16)
    out["wv"] = wkv[:, :, B_NOPE:].astype(BF16)
    out["nq"] = jnp.pad(p["qk_norm_q_g"].astype(F32), (0, QK_PAD - B_DQK)).reshape(1, QK_PAD)
    nk = p["qk_norm_k_g"].astype(F32)
    out["nk_col"] = jnp.pad(nk, (0, QK_PAD - B_DQK)).reshape(QK_PAD, 1)
    out["score_bound"] = (1.01 * B_DQK ** 0.5 * LOG2E * jnp.max(jnp.abs(nk))
                          * jnp.max(jnp.abs(p["qk_norm_q_g"]))).astype(F32).reshape(1, 1)
    out["wpa"] = p["w_proj_a"].astype(BF16)
    out["wpb"] = p["w_proj_b"].astype(BF16)
    out["wo"] = p["w_out"].astype(BF16)
    out["g_ffn"] = p["norm_ffn_g"].astype(F32).reshape(1, D_MODEL)
    out["wpq"] = p["w_peer_q"].astype(BF16)
    out["k1"] = p["peer_keys1"].astype(BF16)
    out["k2"] = p["peer_keys2"].astype(BF16)
    out["u"] = (p["peer_u"] * RSQRT2).astype(BF16)
    out["vt"] = p["peer_v"].T.astype(BF16)
    return out


def _rope_tables(S):
    pos = jnp.arange(S, dtype=F32)
    inv = ROPE_THETA ** (-jnp.arange(0, B_ROPE, 2, dtype=F32) / B_ROPE)
    ang = pos[:, None] * inv[None, :]
    ang = jnp.concatenate([ang, ang], axis=-1)
    sign = jnp.where(jnp.arange(B_ROPE) < B_ROPE // 2, -1.0, 1.0).astype(F32)
    pad = ((0, 0), (0, LANE - B_ROPE))
    cos, sin = jnp.pad(jnp.cos(ang), pad), jnp.pad(jnp.sin(ang) * sign, pad)
    return cos, sin, cos.T, sin.T


def _pick(n, pref):
    t = min(n, pref)
    assert n % t == 0, (n, t)
    return t


def _layer(x, w):
    B, S, _ = x.shape
    T = B * S
    x2 = x.reshape(T, D_MODEL)
    qa, va, osig, cqkv, kg, gs, kt, gt, kb, bt = _inproj(
        x2, w["g_mix"], w["wa"], w["wc"], w["wkg"], w["wg"], w["wkt"], w["wgt"], w["bkg"], w["bgt"],
        _pick(T, 256))
    hf = _mlstm(qa, kt, va, kb, gt, bt, B, S, False)
    ha = _mlstm(qa, kt, va, kb, gt, bt, B, S, True, hf, osig, w["g_mlstm"])
    cos, sin, cost, sint = _rope_tables(S)
    ts = _pick(S, 256)
    nkt = jnp.broadcast_to(w["nk_col"], (QK_PAD, ts))
    q, kbt, v = _mlaprep(cqkv, kg, cos, sin, cost, sint, w["gq"], w["gkv"], w["wq"], w["wkbt"], w["wv"], w["nq"],
                         nkt, w["score_bound"], B, S, ts)
    tk = _pick(S, 512)
    tq = _pick(S, 1024 if S // tk <= 8 else 512)
    ao = lax.cond(w["score_bound"][0, 0] <= 50.0,
                  lambda *a: _attention(*a, tq, tk, True),
                  lambda *a: _attention(*a, tq, tk, False),
                  q, kbt, v).reshape(T, B_HEADS * B_DV)
    x1 = _merge(ha, ao, gs, x2, w["wpa"], w["wpb"], w["wo"], _pick(T, 512))
    xt, p1, p2, th = _route(x1, w["g_ffn"], w["wpq"], w["k1"], w["k2"], _pick(T, 256))
    y = _peer(xt, w["u"], w["vt"], p1, p2, th, x1, _pick(T, 256))
    return y.reshape(B, S, D_MODEL)


def kernel(x_prompt, x_sample, norm_mix_g, w_in, b_mgate, mlstm_norm_g, q_a_norm_g, w_q_up, kv_a_norm_g, w_kv_up, qk_norm_q_g, qk_norm_k_g, w_proj_a, w_proj_b, w_out, norm_ffn_g, w_peer_q, peer_keys1, peer_keys2, peer_u, peer_v):
    params = dict(norm_mix_g=norm_mix_g, w_in=w_in, b_mgate=b_mgate, mlstm_norm_g=mlstm_norm_g,
                  q_a_norm_g=q_a_norm_g, w_q_up=w_q_up, kv_a_norm_g=kv_a_norm_g, w_kv_up=w_kv_up,
                  qk_norm_q_g=qk_norm_q_g, qk_norm_k_g=qk_norm_k_g, w_proj_a=w_proj_a, w_proj_b=w_proj_b,
                  w_out=w_out, norm_ffn_g=norm_ffn_g, w_peer_q=w_peer_q, peer_keys1=peer_keys1,
                  peer_keys2=peer_keys2, peer_u=peer_u, peer_v=peer_v)
    depth = w_in.shape[0]
    layers = [_prep_weights({k: v[l] for k, v in params.items()}) for l in range(depth)]

    def run(x):
        for w in layers:
            x = _layer(x, w)
        return x

    return (run(x_prompt), run(x_sample))
```

```python
import functools

import jax
import jax.numpy as jnp
import numpy as np
from jax import lax
from jax.experimental import pallas as pl
from jax.experimental.pallas import tpu as pltpu

F32 = jnp.float32
BF16 = jnp.bfloat16

D_MODEL = 1024
A_HEADS, A_DQK, A_DV, A_CHUNK = 8, 64, 128, 128
B_HEADS, B_NOPE, B_ROPE, B_DV = 8, 128, 64, 128
B_DQK = B_NOPE + B_ROPE
Q_LORA, KV_LORA = 384, 256
ROPE_THETA = 10000.0
PEER_HEADS, PEER_NKEYS, PEER_DQ, PEER_TOPK = 8, 128, 256, 16
PEER_HALF = PEER_DQ // 2
PEER_EXPERTS = PEER_NKEYS * PEER_NKEYS
EPS = 1e-6

LANE = 128
QK_PAD = 2 * LANE
VMEM_LIMIT = 52 * 1024 * 1024


def _cparams(*sem):
    return pltpu.CompilerParams(dimension_semantics=sem, vmem_limit_bytes=VMEM_LIMIT)


def _dot(a, b):
    return jnp.dot(a, b, preferred_element_type=F32)


def _dot_nt(a, b):
    return lax.dot_general(a, b, (((1,), (1,)), ((), ())), preferred_element_type=F32)


def _rms(x, g):
    return x * lax.rsqrt(jnp.mean(x * x, axis=-1, keepdims=True) + EPS) * g


def _log_sigmoid(x):
    return jnp.minimum(x, 0.0) - jnp.log(1.0 + jnp.exp(-jnp.abs(x)))


def _sigmoid(x):
    return 1.0 / (1.0 + jnp.exp(-x))


def _const_spec(shape):
    nd = len(shape)
    return pl.BlockSpec(shape, lambda *_: (0,) * nd)


def _split3(x):
    hi = x.astype(BF16)
    r = x - hi.astype(F32)
    mid = r.astype(BF16)
    lo = (r - mid.astype(F32)).astype(BF16)
    return hi, mid, lo


def _dot3(a_f32, b_bf16):
    hi, mid, lo = _split3(a_f32)
    return _dot(hi, b_bf16) + _dot(mid, b_bf16) + _dot(lo, b_bf16)


def _dot3r(a_bf16, b_f32):
    hi, mid, lo = _split3(b_f32)
    return _dot(a_bf16, hi) + _dot(a_bf16, mid) + _dot(a_bf16, lo)


def _cummax_rows(y, reverse, period):
    n = y.shape[0]
    row = lax.broadcasted_iota(jnp.int32, y.shape, 0) % period
    k = 1
    while k < period:
        if reverse:
            sh = jnp.where(row < period - k, pltpu.roll(y, n - k, 0), -jnp.inf)
        else:
            sh = jnp.where(row >= k, pltpu.roll(y, k, 0), -jnp.inf)
        y = jnp.maximum(y, sh)
        k *= 2
    return y


def _inproj_kernel(x_ref, g_ref, wa_ref, wc_ref, wkg_ref, wg_ref, wkt_ref, wgt_ref, bkg_ref, bgt_ref,
                   qa_ref, va_ref, os_ref, cqkv_ref, kg_ref, gs_ref, kt_ref, gt_ref, kb_ref, bt_ref):
    h = _rms(x_ref[...], g_ref[...]).astype(BF16)
    a = _dot(h, wa_ref[...])
    qa_ref[...] = a[:, :1024].astype(BF16)
    va_ref[...] = a[:, 1024:2048].astype(BF16)
    os_ref[...] = _sigmoid(a[:, 2048:]).astype(BF16)
    cqkv_ref[...] = _dot(h, wc_ref[...])
    kg = _dot(h, wkg_ref[...]) + bkg_ref[...]
    lane = lax.broadcasted_iota(jnp.int32, kg.shape, 1)
    is_f = ((lane >= 72) & (lane < 80)) | (lane >= 88)
    kg = jnp.where(is_f, _log_sigmoid(kg), kg)
    kg_ref[...] = kg
    gs_ref[...] = _sigmoid(_dot(h, wg_ref[...])).astype(BF16)
    kt_ref[...] = _dot_nt(wkt_ref[...], h).astype(BF16)
    gt = _dot_nt(wgt_ref[...], h) + bgt_ref[...]
    row = lax.broadcasted_iota(jnp.int32, gt.shape, 0)
    is_fr = ((row >= 8) & (row < 16)) | (row >= 24)
    gt = jnp.where(is_fr, _log_sigmoid(gt), gt)
    gt_ref[...] = gt
    tm = kg.shape[0]
    r = lax.broadcasted_iota(jnp.int32, (tm, tm), 0)
    c = lax.broadcasted_iota(jnp.int32, (tm, tm), 1)
    same = (r // A_CHUNK) == (c // A_CHUNK)
    lower = jnp.where(same & (c <= r), 1.0, 0.0).astype(BF16)
    upper = jnp.where(same & (c >= r), 1.0, 0.0).astype(BF16)
    g, gal = kg[:, 64:96], kg[:, 96:128]
    fwd_l = lax.broadcasted_iota(jnp.int32, g.shape, 1) < 2 * A_HEADS
    b_al = jnp.where(fwd_l, _dot3r(lower, gal), _dot3r(upper, gal))
    y = jnp.concatenate([g - b_al, jnp.zeros((tm, LANE - 4 * A_HEADS), F32)], axis=1)
    cm = jnp.where(fwd_l, _cummax_rows(y, False, A_CHUNK)[:, :4 * A_HEADS],
                   _cummax_rows(y, True, A_CHUNK)[:, :4 * A_HEADS])
    kb_ref[...] = jnp.concatenate([b_al, cm, jnp.zeros((tm, LANE - 8 * A_HEADS), F32)], axis=1)
    fwd_r = lax.broadcasted_iota(jnp.int32, gt.shape, 0) < 2 * A_HEADS
    bt_ref[...] = jnp.where(fwd_r, _dot3(gt, upper), _dot3(gt, lower))


def _inproj(x2, g, wa, wc, wkg, wg, wkt, wgt, bkg, bgt, tm):
    T = x2.shape[0]
    row = lambda w: pl.BlockSpec((tm, w), lambda i: (i, 0))
    col = lambda r: pl.BlockSpec((r, tm), lambda i: (0, i))
    return pl.pallas_call(
        _inproj_kernel,
        grid=(T // tm,),
        in_specs=[row(D_MODEL), _const_spec(g.shape), _const_spec(wa.shape), _const_spec(wc.shape),
                  _const_spec(wkg.shape), _const_spec(wg.shape), _const_spec(wkt.shape),
                  _const_spec(wgt.shape), _const_spec(bkg.shape), _const_spec(bgt.shape)],
        out_specs=[row(1024), row(1024), row(1024), row(640), row(128), row(2048), col(1024), col(32), row(128),
                   col(32)],
        out_shape=[jax.ShapeDtypeStruct((T, 1024), BF16), jax.ShapeDtypeStruct((T, 1024), BF16),
                   jax.ShapeDtypeStruct((T, 1024), BF16), jax.ShapeDtypeStruct((T, 640), F32),
                   jax.ShapeDtypeStruct((T, 128), F32), jax.ShapeDtypeStruct((T, 2048), BF16),
                   jax.ShapeDtypeStruct((1024, T), BF16), jax.ShapeDtypeStruct((32, T), F32),
                   jax.ShapeDtypeStruct((T, 128), F32), jax.ShapeDtypeStruct((32, T), F32)],
        compiler_params=_cparams("parallel"),
        name="inproj",
    )(x2, g, wa, wc, wkg, wg, wkt, wgt, bkg, bgt)


def _mlstm_kernel(*refs, reverse):
    if reverse:
        q_ref, kt_ref, v_ref, kb_ref, gt_ref, bt_ref, hf_ref, os_ref, ng_ref, out_ref, cn_scr, m_scr = refs
    else:
        q_ref, kt_ref, v_ref, kb_ref, gt_ref, bt_ref, out_ref, cn_scr, m_scr = refs
    L = A_CHUNK

    @pl.when(pl.program_id(1) == 0)
    def _():
        cn_scr[...] = jnp.zeros_like(cn_scr)
        m_scr[...] = jnp.zeros_like(m_scr)

    row = lax.broadcasted_iota(jnp.int32, (L, L), 0)
    col = lax.broadcasted_iota(jnp.int32, (L, L), 1)
    mask = (col >= row) if reverse else (col <= row)
    gt = gt_ref[...]
    b_rows = bt_ref[...]
    bc3 = jnp.concatenate(_split3(kb_ref[:, :8 * A_HEADS]), axis=1)
    off = 16 if reverse else 0
    ones = jnp.ones((L, LANE), BF16)
    sel_r = lax.broadcasted_iota(jnp.int32, (6 * 4 * A_HEADS, 2 * LANE), 0)
    sel_c = lax.broadcasted_iota(jnp.int32, (6 * 4 * A_HEADS, 2 * LANE), 1)
    sel_r = sel_r % (8 * A_HEADS) - jnp.where(sel_c >= LANE, 4 * A_HEADS, 0)
    last = 0 if reverse else L - 1

    H = range(A_HEADS)
    sls = [slice(h * LANE, (h + 1) * LANE) for h in H]
    bc = [_dot(bc3, jnp.where(sel_r == off + h, 1.0, 0.0).astype(BF16)) for h in H]
    b_c = [x[:, :LANE] for x in bc]
    cm_c = [x[:, LANE:] for x in bc]
    qk = [_dot(q_ref[:, sls[h]], kt_ref[sls[h], :]) for h in H]
    cns = [cn_scr[h] for h in H]
    qc = [_dot(q_ref[:, sls[h]], cns[h].astype(BF16)) for h in H]
    ms = [m_scr[h] for h in H]
    i_r = [gt[off + h:off + h + 1, :] for h in H]
    b_r = [b_rows[off + 8 + h:off + 9 + h, :] for h in H]
    m_t = [b_c[h] + jnp.maximum(cm_c[h], ms[h]) for h in H]
    s = [(qk[h] * jnp.where(mask, jnp.exp(b_c[h] - b_r[h] + i_r[h] - m_t[h]), 0.0)).astype(BF16) for h in H]
    vext = [jnp.concatenate([v_ref[:, sls[h]], ones], axis=1) for h in H]
    sv = [_dot(s[h], vext[h]) for h in H]
    b_last = [b_c[h][last:last + 1] for h in H]
    m_new = [b_last[h] + jnp.maximum(ms[h], cm_c[h][last:last + 1]) for h in H]
    kts = [(kt_ref[sls[h], :].astype(F32) * jnp.exp(b_last[h] - b_r[h] + i_r[h] - m_new[h])).astype(BF16) for h in H]
    kv = [_dot(kts[h], vext[h]) for h in H]
    for h in H:
        decay = jnp.exp(b_last[h] + ms[h] - m_new[h])
        cn_scr[h] = jnp.concatenate([decay, decay], axis=1) * cns[h] + kv[h]
        m_scr[h] = m_new[h]
        inter_w = jnp.exp(b_c[h] + ms[h] - m_t[h])
        tot = sv[h] + jnp.concatenate([inter_w, inter_w], axis=1) * qc[h]
        hh = tot[:, :LANE] / jnp.maximum(jnp.abs(tot[:, LANE:]), jnp.exp(-m_t[h]))
        if reverse:
            t = hh + hf_ref[:, sls[h]]
            y = t * lax.rsqrt(jnp.mean(t * t, axis=-1, keepdims=True) + EPS) * ng_ref[:, sls[h]]
            out_ref[:, sls[h]] = (y * os_ref[:, sls[h]].astype(F32)).astype(BF16)
        else:
            out_ref[:, sls[h]] = hh


def _mlstm(qa, kt, va, kb, gt, bt, B, S, reverse, hf=None, osig=None, ng=None):
    L = A_CHUNK
    nc = S // L
    T = B * S
    if reverse:
        blk = lambda b, c: b * nc + (nc - 1 - c)
    else:
        blk = lambda b, c: b * nc + c
    row = lambda w: pl.BlockSpec((L, w), lambda b, c: (blk(b, c), 0))
    col = lambda r: pl.BlockSpec((r, L), lambda b, c: (0, blk(b, c)))
    in_specs = [row(1024), col(1024), row(1024), row(128), col(32), col(32)]
    args = [qa, kt, va, kb, gt, bt]
    if reverse:
        in_specs += [row(1024), row(1024), _const_spec(ng.shape)]
        args += [hf, osig, ng]
    return pl.pallas_call(
        functools.partial(_mlstm_kernel, reverse=reverse),
        grid=(B, nc),
        in_specs=in_specs,
        out_specs=row(1024),
        out_shape=jax.ShapeDtypeStruct((T, 1024), BF16 if reverse else F32),
        scratch_shapes=[pltpu.VMEM((A_HEADS, LANE, 2 * LANE), F32), pltpu.VMEM((A_HEADS, 1, LANE), F32)],
        compiler_params=_cparams("parallel", "arbitrary"),
        name="mlstm_bwd" if reverse else "mlstm_fwd",
    )(*args)


def _rope(x, cos, sin_signed, axis):
    idx = lax.broadcasted_iota(jnp.int32, x.shape, axis)
    half = B_ROPE // 2
    rot = jnp.where(idx < half, pltpu.roll(x, LANE - half, axis), pltpu.roll(x, half, axis))
    return x * cos + rot * sin_signed


BOUND_COL = B_DQK - LANE
LOG2E = 1.4426950408889634


def _mlaprep_kernel(cqkv_ref, kg_ref, cos_ref, sin_ref, cost_ref, sint_ref, gq_ref, gkv_ref, wq_ref, wkt_ref,
                    wv_ref, nq_ref, nkt_ref, sb_ref, q_ref, kt_ref, v_ref):
    ts = cqkv_ref.shape[0]
    cq = _rms(cqkv_ref[:, :Q_LORA], gq_ref[...]).astype(BF16)
    ckv = _rms(cqkv_ref[:, Q_LORA:], gkv_ref[...]).astype(BF16)
    kgt = kg_ref[...].T
    rowi = lax.broadcasted_iota(jnp.int32, kgt.shape, 0)
    krt = jnp.where(rowi < B_ROPE, kgt, 0.0)
    ss_r = jnp.sum(krt * krt, axis=0, keepdims=True)
    nkt = nkt_ref[...]
    krt_rot = _rope(krt * nkt[LANE:], cost_ref[...], sint_ref[...], 0)
    one_row = jnp.where(rowi == BOUND_COL, 1.0, 0.0)
    nq = nq_ref[...]
    cos, sin = cos_ref[...], sin_ref[...]
    lane = lax.broadcasted_iota(jnp.int32, (ts, LANE), 1)
    scale = B_DQK ** -0.5 * LOG2E
    neg_bound = jnp.broadcast_to(-sb_ref[...], (ts, LANE))
    ones = jnp.ones((ts, B_DV), BF16)
    H = range(B_HEADS)
    qs = [_dot(cq, wq_ref[h]) for h in H]
    kns = [_dot_nt(wkt_ref[h], ckv) for h in H]
    rqs = [lax.rsqrt(jnp.sum(q * q, axis=-1, keepdims=True) * (1.0 / B_DQK) + EPS) * scale for q in qs]
    rks = [lax.rsqrt((jnp.sum(k * k, axis=0, keepdims=True) + ss_r) * (1.0 / B_DQK) + EPS) for k in kns]
    qns = [q * nq for q in qs]
    rots = [_rope(qn[:, LANE:], cos, sin, 1) for qn in qns]
    for h in H:
        q_ref[0, h, :, :LANE] = (qns[h][:, :LANE] * rqs[h]).astype(BF16)
        q_ref[0, h, :, LANE:] = jnp.where(lane == BOUND_COL, neg_bound, rots[h] * rqs[h]).astype(BF16)
        kt_ref[0, h, :LANE, :] = (kns[h] * nkt[:LANE] * rks[h]).astype(BF16)
        kt_ref[0, h, LANE:, :] = (krt_rot * rks[h] + one_row).astype(BF16)
        v_ref[0, h, :, :B_DV] = _dot(ckv, wv_ref[h]).astype(BF16)
        v_ref[0, h, :, B_DV:] = ones


def _mlaprep(cqkv, kg, cos, sin, cost, sint, gq, gkv, wq, wkt, wv, nq, nkt, sb, B, S, ts):
    ns = S // ts
    row = lambda w: pl.BlockSpec((ts, w), lambda b, i: (b * ns + i, 0))
    pos = pl.BlockSpec((ts, LANE), lambda b, i: (i, 0))
    post = pl.BlockSpec((LANE, ts), lambda b, i: (0, i))
    consts = (gq, gkv, wq, wkt, wv, nq, nkt, sb)
    return pl.pallas_call(
        _mlaprep_kernel,
        grid=(B, ns),
        in_specs=[row(640), row(128), pos, pos, post, post] + [_const_spec(a.shape) for a in consts],
        out_specs=[pl.BlockSpec((1, B_HEADS, ts, QK_PAD), lambda b, i: (b, 0, i, 0)),
                   pl.BlockSpec((1, B_HEADS, QK_PAD, ts), lambda b, i: (b, 0, 0, i)),
                   pl.BlockSpec((1, B_HEADS, ts, 2 * B_DV), lambda b, i: (b, 0, i, 0))],
        out_shape=[jax.ShapeDtypeStruct((B, B_HEADS, S, QK_PAD), BF16),
                   jax.ShapeDtypeStruct((B, B_HEADS, QK_PAD, S), BF16),
                   jax.ShapeDtypeStruct((B, B_HEADS, S, 2 * B_DV), BF16)],
        compiler_params=_cparams("parallel", "parallel"),
        name="mlaprep",
    )(cqkv, kg, cos, sin, cost, sint, *consts)


def _attn_kernel(q_ref, kt_ref, v_ref, o_ref, *, tk, nk):
    q = q_ref[0, 0]
    tq = q.shape[0]

    def body(j, carry):
        m, acc = carry
        start = pl.multiple_of(j * tk, tk)
        kt = kt_ref[0, 0, :, pl.ds(start, tk)]
        v = v_ref[0, 0, pl.ds(start, tk), :]
        s = _dot(q, kt)
        m_new = jnp.maximum(m, jnp.max(s, axis=1, keepdims=True))
        p = jnp.exp2(s - m_new)
        acc = jnp.exp2(m - m_new) * acc + _dot(p.astype(BF16), v)
        return m_new, acc

    init = (jnp.full((tq, 1), -jnp.inf, F32), jnp.zeros((tq, 2 * B_DV), F32))
    _, acc = lax.fori_loop(0, nk, body, init)
    o_ref[0] = (acc[:, :B_DV] / acc[:, B_DV:]).astype(BF16)


def _attn_fast_kernel(q_ref, kt_ref, v_ref, o_ref, *, tk, nk, unroll):
    q = q_ref[0, 0]
    tq = q.shape[0]

    def probs(j):
        kt = kt_ref[0, 0, :, pl.ds(pl.multiple_of(j * tk, tk), tk)]
        return jnp.exp2(_dot(q, kt)).astype(BF16)

    def weighted(j, p, acc):
        v = v_ref[0, 0, pl.ds(pl.multiple_of(j * tk, tk), tk), :]
        return acc + _dot(p, v)

    def body(j, carry):
        p, acc = carry
        return probs(j), weighted(j - 1, p, acc)

    p, acc = lax.fori_loop(1, nk, body, (probs(0), jnp.zeros((tq, 2 * B_DV), F32)), unroll=unroll)
    acc = weighted(nk - 1, p, acc)
    o_ref[0] = (acc[:, :B_DV] / acc[:, B_DV:]).astype(BF16)


def _attention(q, kt, v, tq, tk, fast):
    B, H, S, _ = q.shape
    nk = S // tk
    if fast:
        body, name = functools.partial(_attn_fast_kernel, tk=tk, nk=nk, unroll=max(1, min(8, nk - 1))), "attention_fast"
    else:
        body, name = functools.partial(_attn_kernel, tk=tk, nk=nk), "attention"
    return pl.pallas_call(
        body,
        grid=(B, H, S // tq),
        in_specs=[pl.BlockSpec((1, 1, tq, QK_PAD), lambda b, h, i: (b, h, i, 0)),
                  pl.BlockSpec((1, 1, QK_PAD, S), lambda b, h, i: (b, h, 0, 0)),
                  pl.BlockSpec((1, 1, S, 2 * B_DV), lambda b, h, i: (b, h, 0, 0))],
        out_specs=pl.BlockSpec((1, tq, B_DV), lambda b, h, i: (b, i, h)),
        out_shape=jax.ShapeDtypeStruct((B, S, H * B_DV), BF16),
        compiler_params=_cparams("parallel", "parallel", "arbitrary"),
        name=name,
    )(q, kt, v)


def _merge_kernel(ha_ref, ao_ref, gs_ref, x_ref, wpa_ref, wpb_ref, wo_ref, o_ref):
    ya = _dot(ha_ref[...], wpa_ref[...])
    yb = _dot(ao_ref[...], wpb_ref[...])
    merged = gs_ref[:, :D_MODEL].astype(F32) * ya + gs_ref[:, D_MODEL:].astype(F32) * yb
    o_ref[...] = x_ref[...] + _dot(merged.astype(BF16), wo_ref[...])


def _merge(ha, ao, gs, x2, wpa, wpb, wo, tm):
    T = x2.shape[0]
    row = lambda w: pl.BlockSpec((tm, w), lambda i: (i, 0))
    return pl.pallas_call(
        _merge_kernel,
        grid=(T // tm,),
        in_specs=[row(1024), row(1024), row(2048), row(1024)] + [_const_spec(w.shape) for w in (wpa, wpb, wo)],
        out_specs=row(1024),
        out_shape=jax.ShapeDtypeStruct((T, D_MODEL), F32),
        compiler_params=_cparams("parallel"),
        name="merge",
    )(ha, ao, gs, x2, wpa, wpb, wo)


N_EXTRACT = PEER_TOPK + 1
ROUTE_HEADS_PER_ITER = 8


def _top_rows(s, n):
    rows = []
    for _ in range(n):
        m = jnp.max(s, axis=0, keepdims=True)
        rows.append(m)
        s = jnp.where(s == m, -jnp.inf, s)
    return rows


def _stack8(rows, t):
    ri = lax.broadcasted_iota(jnp.int32, (8, t), 0)
    out = jnp.full((8, t), -jnp.inf, F32)
    for k, r in enumerate(rows):
        out = jnp.where(ri == k, r, out)
    return out


def _route_head(h, q_scr, k1, k2, p1_ref, p2_ref, th_ref):
    qh = q_scr[h]
    t = qh.shape[0]
    s1 = _dot_nt(k1, qh[:, :PEER_HALF])
    s2 = _dot_nt(k2, qh[:, PEER_HALF:])
    v1 = _top_rows(s1, N_EXTRACT)
    v2 = _top_rows(s2, N_EXTRACT)
    v2a, v2b, v2c = _stack8(v2[0:8], t), _stack8(v2[8:16], t), _stack8(v2[16:17], t)
    v1b, v1c = _stack8(v1[8:16], t), _stack8(v1[16:17], t)
    cand = jnp.concatenate([v1[i] + v2a for i in range(8)] +
                           [v1[0] + v2b, v1[0] + v2c, v1b + v2[0], v1c + v2[0]], axis=0)
    best = _top_rows(cand, N_EXTRACT)
    z = jnp.zeros_like(best[0])
    for b in best[:PEER_TOPK]:
        z = z + jnp.exp(b - best[0])
    theta = 0.5 * (best[PEER_TOPK - 1] + best[PEER_TOPK])
    m1, m2 = v1[0], v2[0]
    c = RSQRT2 / z
    p1_ref[h] = jnp.exp(s1 - m1) * c
    p2_ref[h] = jnp.exp(s2 - m2)
    th_ref[pl.ds(h, 1), :] = jnp.exp(theta - m1 - m2) * c


def _route_kernel(x_ref, g_ref, wq_ref, k1_ref, k2_ref, xt_ref, p1_ref, p2_ref, th_ref, q_scr):
    xn = _rms(x_ref[...], g_ref[...])
    xt_ref[...] = xn.T.astype(BF16)
    q = _dot(xn.astype(BF16), wq_ref[...])
    for h in range(PEER_HEADS):
        q_scr[h] = q[:, h * PEER_DQ:(h + 1) * PEER_DQ].astype(BF16)
    k1, k2 = k1_ref[...], k2_ref[...]

    def head_group(i, carry):
        for k in range(ROUTE_HEADS_PER_ITER):
            _route_head(ROUTE_HEADS_PER_ITER * i + k, q_scr, k1, k2, p1_ref, p2_ref, th_ref)
        return carry

    lax.fori_loop(0, PEER_HEADS // ROUTE_HEADS_PER_ITER, head_group, 0)


def _route(x1, g, wq, k1, k2, tr):
    T = x1.shape[0]
    return pl.pallas_call(
        _route_kernel,
        grid=(T // tr,),
        in_specs=[pl.BlockSpec((tr, D_MODEL), lambda i: (i, 0)), _const_spec(g.shape), _const_spec(wq.shape),
                  _const_spec(k1.shape), _const_spec(k2.shape)],
        out_specs=[pl.BlockSpec((D_MODEL, tr), lambda i: (0, i)),
                   pl.BlockSpec((PEER_HEADS, PEER_NKEYS, tr), lambda i: (0, 0, i)),
                   pl.BlockSpec((PEER_HEADS, PEER_NKEYS, tr), lambda i: (0, 0, i)),
                   pl.BlockSpec((PEER_HEADS, tr), lambda i: (0, i))],
        out_shape=[jax.ShapeDtypeStruct((D_MODEL, T), BF16),
                   jax.ShapeDtypeStruct((PEER_HEADS, PEER_NKEYS, T), F32),
                   jax.ShapeDtypeStruct((PEER_HEADS, PEER_NKEYS, T), F32),
                   jax.ShapeDtypeStruct((PEER_HEADS, T), F32)],
        scratch_shapes=[pltpu.VMEM((PEER_HEADS, tr, PEER_DQ), BF16)],
        compiler_params=_cparams("parallel"),
        name="route",
    )(x1, g, wq, k1, k2)


E1_PER_STEP = 32
RSQRT2 = 0.7071067811865476


def _peer_kernel(xt_ref, u_ref, vt_ref, p1_ref, p2_ref, th_ref, x1_ref, o_ref, acc_scr, g_scr):
    s = pl.program_id(0)
    nblk = PEER_EXPERTS // (E1_PER_STEP * PEER_NKEYS)

    @pl.when(s == 0)
    def _():
        acc_scr[...] = jnp.zeros_like(acc_scr)
        g_scr[...] = jnp.zeros_like(g_scr)

    @pl.when((lax.rem(s, nblk) == 1) & (s > 1))
    def _():
        o_ref[...] = x1_ref[...] + acc_scr[...].T
        acc_scr[...] = jnp.zeros_like(acc_scr)

    slot = lax.rem(s, 2)
    acc_scr[...] += _dot(vt_ref[...], g_scr[1 - slot])
    xt = xt_ref[...]
    for e in range(E1_PER_STEP):
        sl = slice(e * PEER_NKEYS, (e + 1) * PEER_NKEYS)
        a = _dot(u_ref[sl, :], xt)
        act = a * (1.0 + lax.erf(a))
        w = jnp.zeros_like(a)
        for h in range(PEER_HEADS):
            p = p2_ref[h] * p1_ref[h, e:e + 1, :]
            w = w + jnp.where(p >= th_ref[h:h + 1, :], p, 0.0)
        g_scr[slot, sl, :] = (w * act).astype(BF16)


def _peer(xt, u, vt, p1, p2, th, x1, tb):
    T = x1.shape[0]
    eb = E1_PER_STEP * PEER_NKEYS
    nblk = PEER_EXPERTS // eb
    nt = T // tb
    cur = lambda s: jnp.minimum(s // nblk, nt - 1)
    done = lambda s: jnp.maximum(s - 2, 0) // nblk
    return pl.pallas_call(
        _peer_kernel,
        grid=(nt * nblk + 2,),
        in_specs=[pl.BlockSpec((D_MODEL, tb), lambda s: (0, cur(s))),
                  pl.BlockSpec((eb, D_MODEL), lambda s: (s % nblk, 0)),
                  pl.BlockSpec((D_MODEL, eb), lambda s: (0, (s + nblk - 1) % nblk)),
                  pl.BlockSpec((PEER_HEADS, E1_PER_STEP, tb), lambda s: (0, s % nblk, cur(s))),
                  pl.BlockSpec((PEER_HEADS, PEER_NKEYS, tb), lambda s: (0, 0, cur(s))),
                  pl.BlockSpec((PEER_HEADS, tb), lambda s: (0, cur(s))),
                  pl.BlockSpec((tb, D_MODEL), lambda s: (done(s), 0))],
        out_specs=pl.BlockSpec((tb, D_MODEL), lambda s: (done(s), 0)),
        out_shape=jax.ShapeDtypeStruct((T, D_MODEL), F32),
        scratch_shapes=[pltpu.VMEM((D_MODEL, tb), F32), pltpu.VMEM((2, eb, tb), BF16)],
        compiler_params=_cparams("arbitrary"),
        name="peer",
    )(xt, u, vt, p1, p2, th, x1)


def _prep_weights(p):
    w_in = p["w_in"]
    sizes = (A_HEADS * A_DQK, A_HEADS * A_DQK, A_HEADS * A_DV, A_HEADS * A_DV, 4 * A_HEADS,
             Q_LORA, KV_LORA, B_ROPE, 2 * D_MODEL)
    pts = np.cumsum((0,) + sizes)
    w_q, w_k, w_v, w_o, w_gate, w_cq, w_ckv, w_kr, w_gbr = (w_in[:, pts[i]:pts[i + 1]] for i in range(9))

    def pad_heads(w):
        w = w.reshape(D_MODEL, A_HEADS, A_DQK)
        return jnp.pad(w, ((0, 0), (0, 0), (0, LANE - A_DQK))).reshape(D_MODEL, A_HEADS * LANE)

    out = {}
    out["wa"] = jnp.concatenate([pad_heads(w_q) * (A_DQK ** -0.5), w_v, w_o], axis=1).astype(BF16)
    out["wc"] = jnp.concatenate([w_cq, w_ckv], axis=1).astype(BF16)
    wg4 = w_gate.reshape(D_MODEL, 4, A_HEADS)
    w_fal = jnp.stack([wg4[:, 1], wg4[:, 1], wg4[:, 3], wg4[:, 3]], axis=1).reshape(D_MODEL, 4 * A_HEADS)
    out["wkg"] = jnp.concatenate([w_kr, w_gate, w_fal], axis=1).astype(BF16)
    out["wg"] = w_gbr.astype(BF16)
    out["wkt"] = pad_heads(w_k).T.astype(BF16)
    out["wgt"] = w_gate.T.astype(BF16)
    bias = p["b_mgate"].astype(F32).reshape(4 * A_HEADS)
    b4 = bias.reshape(4, A_HEADS)
    b_fal = jnp.stack([b4[1], b4[1], b4[3], b4[3]]).reshape(4 * A_HEADS)
    out["bkg"] = jnp.concatenate([jnp.zeros((B_ROPE,), F32), bias, b_fal]).reshape(1, LANE)
    out["bgt"] = bias.reshape(4 * A_HEADS, 1)
    out["g_mix"] = p["norm_mix_g"].astype(F32).reshape(1, D_MODEL)
    out["g_mlstm"] = p["mlstm_norm_g"].astype(F32).reshape(1, A_HEADS * A_DV)
    out["gq"] = p["q_a_norm_g"].astype(F32).reshape(1, Q_LORA)
    out["gkv"] = p["kv_a_norm_g"].astype(F32).reshape(1, KV_LORA)
    wq = p["w_q_up"].reshape(Q_LORA, B_HEADS, B_DQK).transpose(1, 0, 2)
    out["wq"] = jnp.pad(wq, ((0, 0), (0, 0), (0, QK_PAD - B_DQK))).astype(BF16)
    wkv = p["w_kv_up"].reshape(KV_LORA, B_HEADS, B_NOPE + B_DV).transpose(1, 0, 2)
    out["wkbt"] = wkv[:, :, :B_NOPE].transpose(0, 2, 1).astype(BF16)
    out["wv"] = wkv[:, :, B_NOPE:].astype(BF16)
    out["nq"] = jnp.pad(p["qk_norm_q_g"].astype(F32), (0, QK_PAD - B_DQK)).reshape(1, QK_PAD)
    nk = p["qk_norm_k_g"].astype(F32)
    out["nk_col"] = jnp.pad(nk, (0, QK_PAD - B_DQK)).reshape(QK_PAD, 1)
    out["score_bound"] = (1.01 * B_DQK ** 0.5 * LOG2E * jnp.max(jnp.abs(nk))
                          * jnp.max(jnp.abs(p["qk_norm_q_g"]))).astype(F32).reshape(1, 1)
    out["score_bound_row"] = jnp.broadcast_to(out["score_bound"], (1, LANE))
    out["wpa"] = p["w_proj_a"].astype(BF16)
    out["wpb"] = p["w_proj_b"].astype(BF16)
    out["wo"] = p["w_out"].astype(BF16)
    out["g_ffn"] = p["norm_ffn_g"].astype(F32).reshape(1, D_MODEL)
    out["wpq"] = p["w_peer_q"].astype(BF16)
    out["k1"] = p["peer_keys1"].astype(BF16)
    out["k2"] = p["peer_keys2"].astype(BF16)
    out["u"] = (p["peer_u"] * RSQRT2).astype(BF16)
    out["vt"] = p["peer_v"].T.astype(BF16)
    return out


def _rope_tables(S):
    pos = jnp.arange(S, dtype=F32)
    inv = ROPE_THETA ** (-jnp.arange(0, B_ROPE, 2, dtype=F32) / B_ROPE)
    ang = pos[:, None] * inv[None, :]
    ang = jnp.concatenate([ang, ang], axis=-1)
    sign = jnp.where(jnp.arange(B_ROPE) < B_ROPE // 2, -1.0, 1.0).astype(F32)
    pad = ((0, 0), (0, LANE - B_ROPE))
    cos, sin = jnp.pad(jnp.cos(ang), pad), jnp.pad(jnp.sin(ang) * sign, pad)
    return cos, sin, cos.T, sin.T


def _pick(n, pref):
    t = min(n, pref)
    assert n % t == 0, (n, t)
    return t


def _layer(x, w):
    B, S, _ = x.shape
    T = B * S
    x2 = x.reshape(T, D_MODEL)
    qa, va, osig, cqkv, kg, gs, kt, gt, kb, bt = _inproj(
        x2, w["g_mix"], w["wa"], w["wc"], w["wkg"], w["wg"], w["wkt"], w["wgt"], w["bkg"], w["bgt"],
        _pick(T, 256))
    hf = _mlstm(qa, kt, va, kb, gt, bt, B, S, False)
    ha = _mlstm(qa, kt, va, kb, gt, bt, B, S, True, hf, osig, w["g_mlstm"])
    cos, sin, cost, sint = _rope_tables(S)
    ts = _pick(S, 256)
    nkt = jnp.broadcast_to(w["nk_col"], (QK_PAD, ts))
    q, kbt, v = _mlaprep(cqkv, kg, cos, sin, cost, sint, w["gq"], w["gkv"], w["wq"], w["wkbt"], w["wv"], w["nq"],
                         nkt, w["score_bound_row"], B, S, ts)
    tk = _pick(S, 512)
    tq = _pick(S, 1024 if S // tk <= 8 else 512)
    ao = lax.cond(w["score_bound"][0, 0] <= 50.0,
                  lambda *a: _attention(*a, tq, tk, True),
                  lambda *a: _attention(*a, tq, tk, False),
                  q, kbt, v).reshape(T, B_HEADS * B_DV)
    x1 = _merge(ha, ao, gs, x2, w["wpa"], w["wpb"], w["wo"], _pick(T, 512))
    xt, p1, p2, th = _route(x1, w["g_ffn"], w["wpq"], w["k1"], w["k2"], _pick(T, 256))
    y = _peer(xt, w["u"], w["vt"], p1, p2, th, x1, _pick(T, 256))
    return y.reshape(B, S, D_MODEL)


def kernel(x_prompt, x_sample, norm_mix_g, w_in, b_mgate, mlstm_norm_g, q_a_norm_g, w_q_up, kv_a_norm_g, w_kv_up, qk_norm_q_g, qk_norm_k_g, w_proj_a, w_proj_b, w_out, norm_ffn_g, w_peer_q, peer_keys1, peer_keys2, peer_u, peer_v):
    params = dict(norm_mix_g=norm_mix_g, w_in=w_in, b_mgate=b_mgate, mlstm_norm_g=mlstm_norm_g,
                  q_a_norm_g=q_a_norm_g, w_q_up=w_q_up, kv_a_norm_g=kv_a_norm_g, w_kv_up=w_kv_up,
                  qk_norm_q_g=qk_norm_q_g, qk_norm_k_g=qk_norm_k_g, w_proj_a=w_proj_a, w_proj_b=w_proj_b,
                  w_out=w_out, norm_ffn_g=norm_ffn_g, w_peer_q=w_peer_q, peer_keys1=peer_keys1,
                  peer_keys2=peer_keys2, peer_u=peer_u, peer_v=peer_v)
    depth = w_in.shape[0]
    layers = [_prep_weights({k: v[l] for k, v in params.items()}) for l in range(depth)]

    def run(x):
        for w in layers:
            x = _layer(x, w)
        return x

    return (run(x_prompt), run(x_sample))
```

```python
import functools

import jax
import jax.numpy as jnp
import numpy as np
from jax import lax
from jax.experimental import pallas as pl
from jax.experimental.pallas import tpu as pltpu

F32 = jnp.float32
BF16 = jnp.bfloat16

D_MODEL = 1024
A_HEADS, A_DQK, A_DV, A_CHUNK = 8, 64, 128, 128
B_HEADS, B_NOPE, B_ROPE, B_DV = 8, 128, 64, 128
B_DQK = B_NOPE + B_ROPE
Q_LORA, KV_LORA = 384, 256
ROPE_THETA = 10000.0
PEER_HEADS, PEER_NKEYS, PEER_DQ, PEER_TOPK = 8, 128, 256, 16
PEER_HALF = PEER_DQ // 2
PEER_EXPERTS = PEER_NKEYS * PEER_NKEYS
EPS = 1e-6

LANE = 128
QK_PAD = 2 * LANE
VMEM_LIMIT = 52 * 1024 * 1024


def _cparams(*sem):
    return pltpu.CompilerParams(dimension_semantics=sem, vmem_limit_bytes=VMEM_LIMIT)


def _dot(a, b):
    return jnp.dot(a, b, preferred_element_type=F32)


def _dot_nt(a, b):
    return lax.dot_general(a, b, (((1,), (1,)), ((), ())), preferred_element_type=F32)


def _rms(x, g):
    return x * lax.rsqrt(jnp.mean(x * x, axis=-1, keepdims=True) + EPS) * g


def _log_sigmoid(x):
    return jnp.minimum(x, 0.0) - jnp.log(1.0 + jnp.exp(-jnp.abs(x)))


def _sigmoid(x):
    return 1.0 / (1.0 + jnp.exp(-x))


def _const_spec(shape):
    nd = len(shape)
    return pl.BlockSpec(shape, lambda *_: (0,) * nd)


def _split3(x):
    hi = x.astype(BF16)
    r = x - hi.astype(F32)
    mid = r.astype(BF16)
    lo = (r - mid.astype(F32)).astype(BF16)
    return hi, mid, lo


def _dot3(a_f32, b_bf16):
    hi, mid, lo = _split3(a_f32)
    return _dot(hi, b_bf16) + _dot(mid, b_bf16) + _dot(lo, b_bf16)


def _dot3r(a_bf16, b_f32):
    hi, mid, lo = _split3(b_f32)
    return _dot(a_bf16, hi) + _dot(a_bf16, mid) + _dot(a_bf16, lo)


def _cummax_rows(y, reverse, period):
    n = y.shape[0]
    row = lax.broadcasted_iota(jnp.int32, y.shape, 0) % period
    k = 1
    while k < period:
        if reverse:
            sh = jnp.where(row < period - k, pltpu.roll(y, n - k, 0), -jnp.inf)
        else:
            sh = jnp.where(row >= k, pltpu.roll(y, k, 0), -jnp.inf)
        y = jnp.maximum(y, sh)
        k *= 2
    return y


def _inproj_kernel(x_ref, g_ref, wa_ref, wc_ref, wkg_ref, wg_ref, wkt_ref, wgt_ref, bkg_ref, bgt_ref,
                   qa_ref, va_ref, os_ref, cqkv_ref, kg_ref, gs_ref, kt_ref, gt_ref, kb_ref, bt_ref):
    h = _rms(x_ref[...], g_ref[...]).astype(BF16)
    a = _dot(h, wa_ref[...])
    qa_ref[...] = a[:, :1024].astype(BF16)
    va_ref[...] = a[:, 1024:2048].astype(BF16)
    os_ref[...] = _sigmoid(a[:, 2048:]).astype(BF16)
    cqkv_ref[...] = _dot(h, wc_ref[...])
    kg = _dot(h, wkg_ref[...]) + bkg_ref[...]
    lane = lax.broadcasted_iota(jnp.int32, kg.shape, 1)
    is_f = ((lane >= 72) & (lane < 80)) | (lane >= 88)
    kg = jnp.where(is_f, _log_sigmoid(kg), kg)
    kg_ref[...] = kg
    gs_ref[...] = _sigmoid(_dot(h, wg_ref[...])).astype(BF16)
    kt_ref[...] = _dot_nt(wkt_ref[...], h).astype(BF16)
    gt = _dot_nt(wgt_ref[...], h) + bgt_ref[...]
    row = lax.broadcasted_iota(jnp.int32, gt.shape, 0)
    is_fr = ((row >= 8) & (row < 16)) | (row >= 24)
    gt = jnp.where(is_fr, _log_sigmoid(gt), gt)
    gt_ref[...] = gt
    tm = kg.shape[0]
    r = lax.broadcasted_iota(jnp.int32, (tm, tm), 0)
    c = lax.broadcasted_iota(jnp.int32, (tm, tm), 1)
    same = (r // A_CHUNK) == (c // A_CHUNK)
    lower = jnp.where(same & (c <= r), 1.0, 0.0).astype(BF16)
    upper = jnp.where(same & (c >= r), 1.0, 0.0).astype(BF16)
    g, gal = kg[:, 64:96], kg[:, 96:128]
    fwd_l = lax.broadcasted_iota(jnp.int32, g.shape, 1) < 2 * A_HEADS
    b_al = jnp.where(fwd_l, _dot3r(lower, gal), _dot3r(upper, gal))
    y = jnp.concatenate([g - b_al, jnp.zeros((tm, LANE - 4 * A_HEADS), F32)], axis=1)
    cm = jnp.where(fwd_l, _cummax_rows(y, False, A_CHUNK)[:, :4 * A_HEADS],
                   _cummax_rows(y, True, A_CHUNK)[:, :4 * A_HEADS])
    kb_ref[...] = jnp.concatenate([b_al, cm, jnp.zeros((tm, LANE - 8 * A_HEADS), F32)], axis=1)
    fwd_r = lax.broadcasted_iota(jnp.int32, gt.shape, 0) < 2 * A_HEADS
    bt_ref[...] = jnp.where(fwd_r, _dot3(gt, upper), _dot3(gt, lower))


def _inproj(x2, g, wa, wc, wkg, wg, wkt, wgt, bkg, bgt, tm):
    T = x2.shape[0]
    row = lambda w: pl.BlockSpec((tm, w), lambda i: (i, 0))
    col = lambda r: pl.BlockSpec((r, tm), lambda i: (0, i))
    return pl.pallas_call(
        _inproj_kernel,
        grid=(T // tm,),
        in_specs=[row(D_MODEL), _const_spec(g.shape), _const_spec(wa.shape), _const_spec(wc.shape),
                  _const_spec(wkg.shape), _const_spec(wg.shape), _const_spec(wkt.shape),
                  _const_spec(wgt.shape), _const_spec(bkg.shape), _const_spec(bgt.shape)],
        out_specs=[row(1024), row(1024), row(1024), row(640), row(128), row(2048), col(1024), col(32), row(128),
                   col(32)],
        out_shape=[jax.ShapeDtypeStruct((T, 1024), BF16), jax.ShapeDtypeStruct((T, 1024), BF16),
                   jax.ShapeDtypeStruct((T, 1024), BF16), jax.ShapeDtypeStruct((T, 640), F32),
                   jax.ShapeDtypeStruct((T, 128), F32), jax.ShapeDtypeStruct((T, 2048), BF16),
                   jax.ShapeDtypeStruct((1024, T), BF16), jax.ShapeDtypeStruct((32, T), F32),
                   jax.ShapeDtypeStruct((T, 128), F32), jax.ShapeDtypeStruct((32, T), F32)],
        compiler_params=_cparams("parallel"),
        name="inproj",
    )(x2, g, wa, wc, wkg, wg, wkt, wgt, bkg, bgt)


def _mlstm_kernel(*refs, reverse):
    if reverse:
        q_ref, kt_ref, v_ref, kb_ref, gt_ref, bt_ref, hf_ref, os_ref, ng_ref, out_ref, cn_scr, m_scr = refs
    else:
        q_ref, kt_ref, v_ref, kb_ref, gt_ref, bt_ref, out_ref, cn_scr, m_scr = refs
    L = A_CHUNK

    @pl.when(pl.program_id(1) == 0)
    def _():
        cn_scr[...] = jnp.zeros_like(cn_scr)
        m_scr[...] = jnp.zeros_like(m_scr)

    row = lax.broadcasted_iota(jnp.int32, (L, L), 0)
    col = lax.broadcasted_iota(jnp.int32, (L, L), 1)
    mask = (col >= row) if reverse else (col <= row)
    gt = gt_ref[...]
    b_rows = bt_ref[...]
    bc3 = jnp.concatenate(_split3(kb_ref[:, :8 * A_HEADS]), axis=1)
    off = 16 if reverse else 0
    ones = jnp.ones((L, LANE), BF16)
    sel_r = lax.broadcasted_iota(jnp.int32, (6 * 4 * A_HEADS, 2 * LANE), 0)
    sel_c = lax.broadcasted_iota(jnp.int32, (6 * 4 * A_HEADS, 2 * LANE), 1)
    sel_r = sel_r % (8 * A_HEADS) - jnp.where(sel_c >= LANE, 4 * A_HEADS, 0)
    last = 0 if reverse else L - 1

    H = range(A_HEADS)
    sls = [slice(h * LANE, (h + 1) * LANE) for h in H]
    bc = [_dot(bc3, jnp.where(sel_r == off + h, 1.0, 0.0).astype(BF16)) for h in H]
    b_c = [x[:, :LANE] for x in bc]
    cm_c = [x[:, LANE:] for x in bc]
    qk = [_dot(q_ref[:, sls[h]], kt_ref[sls[h], :]) for h in H]
    cns = [cn_scr[h] for h in H]
    qc = [_dot(q_ref[:, sls[h]], cns[h].astype(BF16)) for h in H]
    ms = [m_scr[h] for h in H]
    i_r = [gt[off + h:off + h + 1, :] for h in H]
    b_r = [b_rows[off + 8 + h:off + 9 + h, :] for h in H]
    m_t = [b_c[h] + jnp.maximum(cm_c[h], ms[h]) for h in H]
    s = [(qk[h] * jnp.where(mask, jnp.exp(b_c[h] - b_r[h] + i_r[h] - m_t[h]), 0.0)).astype(BF16) for h in H]
    vext = [jnp.concatenate([v_ref[:, sls[h]], ones], axis=1) for h in H]
    sv = [_dot(s[h], vext[h]) for h in H]
    b_last = [b_c[h][last:last + 1] for h in H]
    m_new = [b_last[h] + jnp.maximum(ms[h], cm_c[h][last:last + 1]) for h in H]
    kts = [(kt_ref[sls[h], :].astype(F32) * jnp.exp(b_last[h] - b_r[h] + i_r[h] - m_new[h])).astype(BF16) for h in H]
    kv = [_dot(kts[h], vext[h]) for h in H]
    for h in H:
        decay = jnp.exp(b_last[h] + ms[h] - m_new[h])
        cn_scr[h] = jnp.concatenate([decay, decay], axis=1) * cns[h] + kv[h]
        m_scr[h] = m_new[h]
        inter_w = jnp.exp(b_c[h] + ms[h] - m_t[h])
        tot = sv[h] + jnp.concatenate([inter_w, inter_w], axis=1) * qc[h]
        hh = tot[:, :LANE] / jnp.maximum(jnp.abs(tot[:, LANE:]), jnp.exp(-m_t[h]))
        if reverse:
            t = hh + hf_ref[:, sls[h]]
            y = t * lax.rsqrt(jnp.mean(t * t, axis=-1, keepdims=True) + EPS) * ng_ref[:, sls[h]]
            out_ref[:, sls[h]] = (y * os_ref[:, sls[h]].astype(F32)).astype(BF16)
        else:
            out_ref[:, sls[h]] = hh


def _mlstm(qa, kt, va, kb, gt, bt, B, S, reverse, hf=None, osig=None, ng=None):
    L = A_CHUNK
    nc = S // L
    T = B * S
    if reverse:
        blk = lambda b, c: b * nc + (nc - 1 - c)
    else:
        blk = lambda b, c: b * nc + c
    row = lambda w: pl.BlockSpec((L, w), lambda b, c: (blk(b, c), 0))
    col = lambda r: pl.BlockSpec((r, L), lambda b, c: (0, blk(b, c)))
    in_specs = [row(1024), col(1024), row(1024), row(128), col(32), col(32)]
    args = [qa, kt, va, kb, gt, bt]
    if reverse:
        in_specs += [row(1024), row(1024), _const_spec(ng.shape)]
        args += [hf, osig, ng]
    return pl.pallas_call(
        functools.partial(_mlstm_kernel, reverse=reverse),
        grid=(B, nc),
        in_specs=in_specs,
        out_specs=row(1024),
        out_shape=jax.ShapeDtypeStruct((T, 1024), BF16 if reverse else F32),
        scratch_shapes=[pltpu.VMEM((A_HEADS, LANE, 2 * LANE), F32), pltpu.VMEM((A_HEADS, 1, LANE), F32)],
        compiler_params=_cparams("parallel", "arbitrary"),
        name="mlstm_bwd" if reverse else "mlstm_fwd",
    )(*args)


def _rope(x, cos, sin_signed, axis):
    idx = lax.broadcasted_iota(jnp.int32, x.shape, axis)
    half = B_ROPE // 2
    rot = jnp.where(idx < half, pltpu.roll(x, LANE - half, axis), pltpu.roll(x, half, axis))
    return x * cos + rot * sin_signed


BOUND_COL = B_DQK - LANE
LOG2E = 1.4426950408889634


def _mlaprep_kernel(cqkv_ref, kg_ref, cos_ref, sin_ref, cost_ref, sint_ref, gq_ref, gkv_ref, wq_ref, wkt_ref,
                    wv_ref, nq_ref, nkt_ref, sb_ref, q_ref, kt_ref, v_ref):
    ts = cqkv_ref.shape[0]
    cq = _rms(cqkv_ref[:, :Q_LORA], gq_ref[...]).astype(BF16)
    ckv = _rms(cqkv_ref[:, Q_LORA:], gkv_ref[...]).astype(BF16)
    kgt = kg_ref[...].T
    rowi = lax.broadcasted_iota(jnp.int32, kgt.shape, 0)
    krt = jnp.where(rowi < B_ROPE, kgt, 0.0)
    ss_r = jnp.sum(krt * krt, axis=0, keepdims=True)
    nkt = nkt_ref[...]
    krt_rot = _rope(krt * nkt[LANE:], cost_ref[...], sint_ref[...], 0)
    one_row = jnp.where(rowi == BOUND_COL, 1.0, 0.0)
    nq = nq_ref[...]
    cos, sin = cos_ref[...], sin_ref[...]
    lane = lax.broadcasted_iota(jnp.int32, (ts, LANE), 1)
    scale = B_DQK ** -0.5 * LOG2E
    neg_bound = jnp.broadcast_to(-sb_ref[...], (ts, LANE))
    ones = jnp.ones((ts, B_DV), BF16)
    H = range(B_HEADS)
    qs = [_dot(cq, wq_ref[h]) for h in H]
    kns = [_dot_nt(wkt_ref[h], ckv) for h in H]
    rqs = [lax.rsqrt(jnp.sum(q * q, axis=-1, keepdims=True) * (1.0 / B_DQK) + EPS) * scale for q in qs]
    rks = [lax.rsqrt((jnp.sum(k * k, axis=0, keepdims=True) + ss_r) * (1.0 / B_DQK) + EPS) for k in kns]
    qns = [q * nq for q in qs]
    rots = [_rope(qn[:, LANE:], cos, sin, 1) for qn in qns]
    for h in H:
        q_ref[0, h, :, :LANE] = (qns[h][:, :LANE] * rqs[h]).astype(BF16)
        q_ref[0, h, :, LANE:] = jnp.where(lane == BOUND_COL, neg_bound, rots[h] * rqs[h]).astype(BF16)
        kt_ref[0, h, :LANE, :] = (kns[h] * nkt[:LANE] * rks[h]).astype(BF16)
        kt_ref[0, h, LANE:, :] = (krt_rot * rks[h] + one_row).astype(BF16)
        v_ref[0, h, :, :B_DV] = _dot(ckv, wv_ref[h]).astype(BF16)
        v_ref[0, h, :, B_DV:] = ones


def _mlaprep(cqkv, kg, cos, sin, cost, sint, gq, gkv, wq, wkt, wv, nq, nkt, sb, B, S, ts):
    ns = S // ts
    row = lambda w: pl.BlockSpec((ts, w), lambda b, i: (b * ns + i, 0))
    pos = pl.BlockSpec((ts, LANE), lambda b, i: (i, 0))
    post = pl.BlockSpec((LANE, ts), lambda b, i: (0, i))
    consts = (gq, gkv, wq, wkt, wv, nq, nkt, sb)
    return pl.pallas_call(
        _mlaprep_kernel,
        grid=(B, ns),
        in_specs=[row(640), row(128), pos, pos, post, post] + [_const_spec(a.shape) for a in consts],
        out_specs=[pl.BlockSpec((1, B_HEADS, ts, QK_PAD), lambda b, i: (b, 0, i, 0)),
                   pl.BlockSpec((1, B_HEADS, QK_PAD, ts), lambda b, i: (b, 0, 0, i)),
                   pl.BlockSpec((1, B_HEADS, ts, 2 * B_DV), lambda b, i: (b, 0, i, 0))],
        out_shape=[jax.ShapeDtypeStruct((B, B_HEADS, S, QK_PAD), BF16),
                   jax.ShapeDtypeStruct((B, B_HEADS, QK_PAD, S), BF16),
                   jax.ShapeDtypeStruct((B, B_HEADS, S, 2 * B_DV), BF16)],
        compiler_params=_cparams("parallel", "parallel"),
        name="mlaprep",
    )(cqkv, kg, cos, sin, cost, sint, *consts)


def _attn_kernel(q_ref, kt_ref, v_ref, o_ref, *, tk, nk):
    q = q_ref[0, 0]
    tq = q.shape[0]

    def body(j, carry):
        m, acc = carry
        start = pl.multiple_of(j * tk, tk)
        kt = kt_ref[0, 0, :, pl.ds(start, tk)]
        v = v_ref[0, 0, pl.ds(start, tk), :]
        s = _dot(q, kt)
        m_new = jnp.maximum(m, jnp.max(s, axis=1, keepdims=True))
        p = jnp.exp2(s - m_new)
        acc = jnp.exp2(m - m_new) * acc + _dot(p.astype(BF16), v)
        return m_new, acc

    init = (jnp.full((tq, 1), -jnp.inf, F32), jnp.zeros((tq, 2 * B_DV), F32))
    _, acc = lax.fori_loop(0, nk, body, init)
    o_ref[0] = (acc[:, :B_DV] / acc[:, B_DV:]).astype(BF16)


def _attn_fast_kernel(q_ref, kt_ref, v_ref, o_ref, *, tk, nk, unroll):
    q = q_ref[0, 0]
    tq = q.shape[0]

    def probs(j):
        kt = kt_ref[0, 0, :, pl.ds(pl.multiple_of(j * tk, tk), tk)]
        return jnp.exp2(_dot(q, kt)).astype(BF16)

    def weighted(j, p, acc):
        v = v_ref[0, 0, pl.ds(pl.multiple_of(j * tk, tk), tk), :]
        return acc + _dot(p, v)

    def body(j, carry):
        p, acc = carry
        return probs(j), weighted(j - 1, p, acc)

    p, acc = lax.fori_loop(1, nk, body, (probs(0), jnp.zeros((tq, 2 * B_DV), F32)), unroll=unroll)
    acc = weighted(nk - 1, p, acc)
    o_ref[0] = (acc[:, :B_DV] / acc[:, B_DV:]).astype(BF16)


def _attention(q, kt, v, tq, tk, fast):
    B, H, S, _ = q.shape
    nk = S // tk
    if fast:
        body, name = functools.partial(_attn_fast_kernel, tk=tk, nk=nk, unroll=max(1, min(8, nk - 1))), "attention_fast"
    else:
        body, name = functools.partial(_attn_kernel, tk=tk, nk=nk), "attention"
    return pl.pallas_call(
        body,
        grid=(B, H, S // tq),
        in_specs=[pl.BlockSpec((1, 1, tq, QK_PAD), lambda b, h, i: (b, h, i, 0)),
                  pl.BlockSpec((1, 1, QK_PAD, S), lambda b, h, i: (b, h, 0, 0)),
                  pl.BlockSpec((1, 1, S, 2 * B_DV), lambda b, h, i: (b, h, 0, 0))],
        out_specs=pl.BlockSpec((1, tq, B_DV), lambda b, h, i: (b, i, h)),
        out_shape=jax.ShapeDtypeStruct((B, S, H * B_DV), BF16),
        compiler_params=_cparams("parallel", "parallel", "arbitrary"),
        name=name,
    )(q, kt, v)


def _merge_kernel(ha_ref, ao_ref, gs_ref, x_ref, wpa_ref, wpb_ref, wo_ref, o_ref):
    ya = _dot(ha_ref[...], wpa_ref[...])
    yb = _dot(ao_ref[...], wpb_ref[...])
    merged = gs_ref[:, :D_MODEL].astype(F32) * ya + gs_ref[:, D_MODEL:].astype(F32) * yb
    o_ref[...] = x_ref[...] + _dot(merged.astype(BF16), wo_ref[...])


def _merge(ha, ao, gs, x2, wpa, wpb, wo, tm):
    T = x2.shape[0]
    row = lambda w: pl.BlockSpec((tm, w), lambda i: (i, 0))
    return pl.pallas_call(
        _merge_kernel,
        grid=(T // tm,),
        in_specs=[row(1024), row(1024), row(2048), row(1024)] + [_const_spec(w.shape) for w in (wpa, wpb, wo)],
        out_specs=row(1024),
        out_shape=jax.ShapeDtypeStruct((T, D_MODEL), F32),
        compiler_params=_cparams("parallel"),
        name="merge",
    )(ha, ao, gs, x2, wpa, wpb, wo)


N_EXTRACT = PEER_TOPK + 1
ROUTE_HEADS_PER_ITER = 8


def _top_rows(s, n):
    rows = []
    for _ in range(n):
        m = jnp.max(s, axis=0, keepdims=True)
        rows.append(m)
        s = jnp.where(s == m, -jnp.inf, s)
    return rows


def _stack8(rows, t):
    ri = lax.broadcasted_iota(jnp.int32, (8, t), 0)
    out = jnp.full((8, t), -jnp.inf, F32)
    for k, r in enumerate(rows):
        out = jnp.where(ri == k, r, out)
    return out


def _route_head(h, q_scr, k1, k2, p1_ref, p2_ref, th_ref):
    qh = q_scr[h]
    t = qh.shape[0]
    s1 = _dot_nt(k1, qh[:, :PEER_HALF])
    s2 = _dot_nt(k2, qh[:, PEER_HALF:])
    v1 = _top_rows(s1, N_EXTRACT)
    v2 = _top_rows(s2, N_EXTRACT)
    v2a, v2b, v2c = _stack8(v2[0:8], t), _stack8(v2[8:16], t), _stack8(v2[16:17], t)
    v1b, v1c = _stack8(v1[8:16], t), _stack8(v1[16:17], t)
    cand = jnp.concatenate([v1[i] + v2a for i in range(8)] +
                           [v1[0] + v2b, v1[0] + v2c, v1b + v2[0], v1c + v2[0]], axis=0)
    best = _top_rows(cand, N_EXTRACT)
    z = jnp.zeros_like(best[0])
    for b in best[:PEER_TOPK]:
        z = z + jnp.exp(b - best[0])
    theta = 0.5 * (best[PEER_TOPK - 1] + best[PEER_TOPK])
    m1, m2 = v1[0], v2[0]
    c = RSQRT2 / z
    p1_ref[h] = jnp.exp(s1 - m1) * c
    p2_ref[h] = jnp.exp(s2 - m2)
    th_ref[pl.ds(h, 1), :] = jnp.exp(theta - m1 - m2) * c


def _route_kernel(x_ref, g_ref, wq_ref, k1_ref, k2_ref, xt_ref, p1_ref, p2_ref, th_ref, q_scr):
    xn = _rms(x_ref[...], g_ref[...])
    xt_ref[...] = xn.T.astype(BF16)
    q = _dot(xn.astype(BF16), wq_ref[...])
    for h in range(PEER_HEADS):
        q_scr[h] = q[:, h * PEER_DQ:(h + 1) * PEER_DQ].astype(BF16)
    k1, k2 = k1_ref[...], k2_ref[...]

    def head_group(i, carry):
        for k in range(ROUTE_HEADS_PER_ITER):
            _route_head(ROUTE_HEADS_PER_ITER * i + k, q_scr, k1, k2, p1_ref, p2_ref, th_ref)
        return carry

    lax.fori_loop(0, PEER_HEADS // ROUTE_HEADS_PER_ITER, head_group, 0)


def _route(x1, g, wq, k1, k2, tr):
    T = x1.shape[0]
    return pl.pallas_call(
        _route_kernel,
        grid=(T // tr,),
        in_specs=[pl.BlockSpec((tr, D_MODEL), lambda i: (i, 0)), _const_spec(g.shape), _const_spec(wq.shape),
                  _const_spec(k1.shape), _const_spec(k2.shape)],
        out_specs=[pl.BlockSpec((D_MODEL, tr), lambda i: (0, i)),
                   pl.BlockSpec((PEER_HEADS, PEER_NKEYS, tr), lambda i: (0, 0, i)),
                   pl.BlockSpec((PEER_HEADS, PEER_NKEYS, tr), lambda i: (0, 0, i)),
                   pl.BlockSpec((PEER_HEADS, tr), lambda i: (0, i))],
        out_shape=[jax.ShapeDtypeStruct((D_MODEL, T), BF16),
                   jax.ShapeDtypeStruct((PEER_HEADS, PEER_NKEYS, T), F32),
                   jax.ShapeDtypeStruct((PEER_HEADS, PEER_NKEYS, T), F32),
                   jax.ShapeDtypeStruct((PEER_HEADS, T), F32)],
        scratch_shapes=[pltpu.VMEM((PEER_HEADS, tr, PEER_DQ), BF16)],
        compiler_params=_cparams("parallel"),
        name="route",
    )(x1, g, wq, k1, k2)


E1_PER_STEP = 32
RSQRT2 = 0.7071067811865476


def _peer_kernel(xt_ref, u_ref, vt_ref, p1_ref, p2_ref, th_ref, x1_ref, o_ref, acc_scr, g_scr):
    s = pl.program_id(0)
    nblk = PEER_EXPERTS // (E1_PER_STEP * PEER_NKEYS)

    @pl.when(s == 0)
    def _():
        acc_scr[...] = jnp.zeros_like(acc_scr)
        g_scr[...] = jnp.zeros_like(g_scr)

    @pl.when((lax.rem(s, nblk) == 1) & (s > 1))
    def _():
        o_ref[...] = x1_ref[...] + acc_scr[...].T
        acc_scr[...] = jnp.zeros_like(acc_scr)

    slot = lax.rem(s, 2)
    acc_scr[...] += _dot(vt_ref[...], g_scr[1 - slot])
    xt = xt_ref[...]
    for e in range(E1_PER_STEP):
        sl = slice(e * PEER_NKEYS, (e + 1) * PEER_NKEYS)
        a = _dot(u_ref[sl, :], xt)
        act = a * (1.0 + lax.erf(a))
        w = jnp.zeros_like(a)
        for h in range(PEER_HEADS):
            p = p2_ref[h] * p1_ref[h, e:e + 1, :]
            w = w + jnp.where(p >= th_ref[h:h + 1, :], p, 0.0)
        g_scr[slot, sl, :] = (w * act).astype(BF16)


def _peer(xt, u, vt, p1, p2, th, x1, tb):
    T = x1.shape[0]
    eb = E1_PER_STEP * PEER_NKEYS
    nblk = PEER_EXPERTS // eb
    nt = T // tb
    cur = lambda s: jnp.minimum(s // nblk, nt - 1)
    done = lambda s: jnp.maximum(s - 2, 0) // nblk
    return pl.pallas_call(
        _peer_kernel,
        grid=(nt * nblk + 2,),
        in_specs=[pl.BlockSpec((D_MODEL, tb), lambda s: (0, cur(s))),
                  pl.BlockSpec((eb, D_MODEL), lambda s: (s % nblk, 0)),
                  pl.BlockSpec((D_MODEL, eb), lambda s: (0, (s + nblk - 1) % nblk)),
                  pl.BlockSpec((PEER_HEADS, E1_PER_STEP, tb), lambda s: (0, s % nblk, cur(s))),
                  pl.BlockSpec((PEER_HEADS, PEER_NKEYS, tb), lambda s: (0, 0, cur(s))),
                  pl.BlockSpec((PEER_HEADS, tb), lambda s: (0, cur(s))),
                  pl.BlockSpec((tb, D_MODEL), lambda s: (done(s), 0))],
        out_specs=pl.BlockSpec((tb, D_MODEL), lambda s: (done(s), 0)),
        out_shape=jax.ShapeDtypeStruct((T, D_MODEL), F32),
        scratch_shapes=[pltpu.VMEM((D_MODEL, tb), F32), pltpu.VMEM((2, eb, tb), BF16)],
        compiler_params=_cparams("arbitrary"),
        name="peer",
    )(xt, u, vt, p1, p2, th, x1)


def _prep_weights(p):
    w_in = p["w_in"]
    sizes = (A_HEADS * A_DQK, A_HEADS * A_DQK, A_HEADS * A_DV, A_HEADS * A_DV, 4 * A_HEADS,
             Q_LORA, KV_LORA, B_ROPE, 2 * D_MODEL)
    pts = np.cumsum((0,) + sizes)
    w_q, w_k, w_v, w_o, w_gate, w_cq, w_ckv, w_kr, w_gbr = (w_in[:, pts[i]:pts[i + 1]] for i in range(9))

    def pad_heads(w):
        w = w.reshape(D_MODEL, A_HEADS, A_DQK)
        return jnp.pad(w, ((0, 0), (0, 0), (0, LANE - A_DQK))).reshape(D_MODEL, A_HEADS * LANE)

    out = {}
    out["wa"] = jnp.concatenate([pad_heads(w_q) * (A_DQK ** -0.5), w_v, w_o], axis=1).astype(BF16)
    out["wc"] = jnp.concatenate([w_cq, w_ckv], axis=1).astype(BF16)
    wg4 = w_gate.reshape(D_MODEL, 4, A_HEADS)
    w_fal = jnp.stack([wg4[:, 1], wg4[:, 1], wg4[:, 3], wg4[:, 3]], axis=1).reshape(D_MODEL, 4 * A_HEADS)
    out["wkg"] = jnp.concatenate([w_kr, w_gate, w_fal], axis=1).astype(BF16)
    out["wg"] = w_gbr.astype(BF16)
    out["wkt"] = pad_heads(w_k).T.astype(BF16)
    out["wgt"] = w_gate.T.astype(BF16)
    bias = p["b_mgate"].astype(F32).reshape(4 * A_HEADS)
    b4 = bias.reshape(4, A_HEADS)
    b_fal = jnp.stack([b4[1], b4[1], b4[3], b4[3]]).reshape(4 * A_HEADS)
    out["bkg"] = jnp.concatenate([jnp.zeros((B_ROPE,), F32), bias, b_fal]).reshape(1, LANE)
    out["bgt"] = bias.reshape(4 * A_HEADS, 1)
    out["g_mix"] = p["norm_mix_g"].astype(F32).reshape(1, D_MODEL)
    out["g_mlstm"] = p["mlstm_norm_g"].astype(F32).reshape(1, A_HEADS * A_DV)
    out["gq"] = p["q_a_norm_g"].astype(F32).reshape(1, Q_LORA)
    out["gkv"] = p["kv_a_norm_g"].astype(F32).reshape(1, KV_LORA)
    wq = p["w_q_up"].reshape(Q_LORA, B_HEADS, B_DQK).transpose(1, 0, 2)
    out["wq"] = jnp.pad(wq, ((0, 0), (0, 0), (0, QK_PAD - B_DQK))).astype(BF16)
    wkv = p["w_kv_up"].reshape(KV_LORA, B_HEADS, B_NOPE + B_DV).transpose(1, 0, 2)
    out["wkbt"] = wkv[:, :, :B_NOPE].transpose(0, 2, 1).astype(BF16)
    out["wv"] = wkv[:, :, B_NOPE:].astype(BF16)
    out["nq"] = jnp.pad(p["qk_norm_q_g"].astype(F32), (0, QK_PAD - B_DQK)).reshape(1, QK_PAD)
    nk = p["qk_norm_k_g"].astype(F32)
    out["nk_col"] = jnp.pad(nk, (0, QK_PAD - B_DQK)).reshape(QK_PAD, 1)
    out["score_bound"] = (1.01 * B_DQK ** 0.5 * LOG2E * jnp.max(jnp.abs(nk))
                          * jnp.max(jnp.abs(p["qk_norm_q_g"]))).astype(F32).reshape(1, 1)
    out["score_bound_row"] = jnp.broadcast_to(out["score_bound"], (1, LANE))
    out["wpa"] = p["w_proj_a"].astype(BF16)
    out["wpb"] = p["w_proj_b"].astype(BF16)
    out["wo"] = p["w_out"].astype(BF16)
    out["g_ffn"] = p["norm_ffn_g"].astype(F32).reshape(1, D_MODEL)
    out["wpq"] = p["w_peer_q"].astype(BF16)
    out["k1"] = p["peer_keys1"].astype(BF16)
    out["k2"] = p["peer_keys2"].astype(BF16)
    out["u"] = (p["peer_u"] * RSQRT2).astype(BF16)
    out["vt"] = p["peer_v"].T.astype(BF16)
    return out


def _rope_tables(S):
    pos = jnp.arange(S, dtype=F32)
    inv = ROPE_THETA ** (-jnp.arange(0, B_ROPE, 2, dtype=F32) / B_ROPE)
    ang = pos[:, None] * inv[None, :]
    ang = jnp.concatenate([ang, ang], axis=-1)
    sign = jnp.where(jnp.arange(B_ROPE) < B_ROPE // 2, -1.0, 1.0).astype(F32)
    pad = ((0, 0), (0, LANE - B_ROPE))
    cos, sin = jnp.pad(jnp.cos(ang), pad), jnp.pad(jnp.sin(ang) * sign, pad)
    return cos, sin, cos.T, sin.T


def _pick(n, pref):
    t = min(n, pref)
    assert n % t == 0, (n, t)
    return t


def _layer(x, w):
    B, S, _ = x.shape
    T = B * S
    x2 = x.reshape(T, D_MODEL)
    qa, va, osig, cqkv, kg, gs, kt, gt, kb, bt = _inproj(
        x2, w["g_mix"], w["wa"], w["wc"], w["wkg"], w["wg"], w["wkt"], w["wgt"], w["bkg"], w["bgt"],
        _pick(T, 256))
    hf = _mlstm(qa, kt, va, kb, gt, bt, B, S, False)
    ha = _mlstm(qa, kt, va, kb, gt, bt, B, S, True, hf, osig, w["g_mlstm"])
    cos, sin, cost, sint = _rope_tables(S)
    ts = _pick(S, 256)
    nkt = jnp.broadcast_to(w["nk_col"], (QK_PAD, ts))
    q, kbt, v = _mlaprep(cqkv, kg, cos, sin, cost, sint, w["gq"], w["gkv"], w["wq"], w["wkbt"], w["wv"], w["nq"],
                         nkt, w["score_bound_row"], B, S, ts)
    tq, tk = _pick(S, 1024), _pick(S, 512)
    ao = lax.cond(w["score_bound"][0, 0] <= 50.0,
                  lambda *a: _attention(*a, tq, tk, True),
                  lambda *a: _attention(*a, tq, tk, False),
                  q, kbt, v).reshape(T, B_HEADS * B_DV)
    x1 = _merge(ha, ao, gs, x2, w["wpa"], w["wpb"], w["wo"], _pick(T, 512))
    xt, p1, p2, th = _route(x1, w["g_ffn"], w["wpq"], w["k1"], w["k2"], _pick(T, 256))
    y = _peer(xt, w["u"], w["vt"], p1, p2, th, x1, _pick(T, 256))
    return y.reshape(B, S, D_MODEL)


def kernel(x_prompt, x_sample, norm_mix_g, w_in, b_mgate, mlstm_norm_g, q_a_norm_g, w_q_up, kv_a_norm_g, w_kv_up, qk_norm_q_g, qk_norm_k_g, w_proj_a, w_proj_b, w_out, norm_ffn_g, w_peer_q, peer_keys1, peer_keys2, peer_u, peer_v):
    params = dict(norm_mix_g=norm_mix_g, w_in=w_in, b_mgate=b_mgate, mlstm_norm_g=mlstm_norm_g,
                  q_a_norm_g=q_a_norm_g, w_q_up=w_q_up, kv_a_norm_g=kv_a_norm_g, w_kv_up=w_kv_up,
                  qk_norm_q_g=qk_norm_q_g, qk_norm_k_g=qk_norm_k_g, w_proj_a=w_proj_a, w_proj_b=w_proj_b,
                  w_out=w_out, norm_ffn_g=norm_ffn_g, w_peer_q=w_peer_q, peer_keys1=peer_keys1,
                  peer_keys2=peer_keys2, peer_u=peer_u, peer_v=peer_v)
    depth = w_in.shape[0]
    layers = [_prep_weights({k: v[l] for k, v in params.items()}) for l in range(depth)]

    def run(x):
        for w in layers:
            x = _layer(x, w)
        return x

    return (run(x_prompt), run(x_sample))
```

```python
import functools

import jax
import jax.numpy as jnp
import numpy as np
from jax import lax
from jax.experimental import pallas as pl
from jax.experimental.pallas import tpu as pltpu

F32 = jnp.float32
BF16 = jnp.bfloat16

D_MODEL = 1024
A_HEADS, A_DQK, A_DV, A_CHUNK = 8, 64, 128, 128
B_HEADS, B_NOPE, B_ROPE, B_DV = 8, 128, 64, 128
B_DQK = B_NOPE + B_ROPE
Q_LORA, KV_LORA = 384, 256
ROPE_THETA = 10000.0
PEER_HEADS, PEER_NKEYS, PEER_DQ, PEER_TOPK = 8, 128, 256, 16
PEER_HALF = PEER_DQ // 2
PEER_EXPERTS = PEER_NKEYS * PEER_NKEYS
EPS = 1e-6

LANE = 128
QK_PAD = 2 * LANE
VMEM_LIMIT = 52 * 1024 * 1024


def _cparams(*sem):
    return pltpu.CompilerParams(dimension_semantics=sem, vmem_limit_bytes=VMEM_LIMIT)


def _dot(a, b):
    return jnp.dot(a, b, preferred_element_type=F32)


def _dot_nt(a, b):
    return lax.dot_general(a, b, (((1,), (1,)), ((), ())), preferred_element_type=F32)


def _rms(x, g):
    return x * lax.rsqrt(jnp.mean(x * x, axis=-1, keepdims=True) + EPS) * g


def _log_sigmoid(x):
    return jnp.minimum(x, 0.0) - jnp.log(1.0 + jnp.exp(-jnp.abs(x)))


def _sigmoid(x):
    return 1.0 / (1.0 + jnp.exp(-x))


def _const_spec(shape):
    nd = len(shape)
    return pl.BlockSpec(shape, lambda *_: (0,) * nd)


def _split3(x):
    hi = x.astype(BF16)
    r = x - hi.astype(F32)
    mid = r.astype(BF16)
    lo = (r - mid.astype(F32)).astype(BF16)
    return hi, mid, lo


def _dot3(a_f32, b_bf16):
    hi, mid, lo = _split3(a_f32)
    return _dot(hi, b_bf16) + _dot(mid, b_bf16) + _dot(lo, b_bf16)


def _dot3r(a_bf16, b_f32):
    hi, mid, lo = _split3(b_f32)
    return _dot(a_bf16, hi) + _dot(a_bf16, mid) + _dot(a_bf16, lo)


def _cummax_rows(y, reverse, period):
    n = y.shape[0]
    row = lax.broadcasted_iota(jnp.int32, y.shape, 0) % period
    k = 1
    while k < period:
        if reverse:
            sh = jnp.where(row < period - k, pltpu.roll(y, n - k, 0), -jnp.inf)
        else:
            sh = jnp.where(row >= k, pltpu.roll(y, k, 0), -jnp.inf)
        y = jnp.maximum(y, sh)
        k *= 2
    return y


def _inproj_kernel(x_ref, g_ref, wa_ref, wc_ref, wkg_ref, wg_ref, wkt_ref, wgt_ref, bkg_ref, bgt_ref,
                   qa_ref, va_ref, os_ref, cqkv_ref, kg_ref, gs_ref, kt_ref, gt_ref, kb_ref, bt_ref):
    h = _rms(x_ref[...], g_ref[...]).astype(BF16)
    a = _dot(h, wa_ref[...])
    qa_ref[...] = a[:, :1024].astype(BF16)
    va_ref[...] = a[:, 1024:2048].astype(BF16)
    os_ref[...] = _sigmoid(a[:, 2048:]).astype(BF16)
    cqkv_ref[...] = _dot(h, wc_ref[...])
    kg = _dot(h, wkg_ref[...]) + bkg_ref[...]
    lane = lax.broadcasted_iota(jnp.int32, kg.shape, 1)
    is_f = ((lane >= 72) & (lane < 80)) | (lane >= 88)
    kg = jnp.where(is_f, _log_sigmoid(kg), kg)
    kg_ref[...] = kg
    gs_ref[...] = _sigmoid(_dot(h, wg_ref[...])).astype(BF16)
    kt_ref[...] = _dot_nt(wkt_ref[...], h).astype(BF16)
    gt = _dot_nt(wgt_ref[...], h) + bgt_ref[...]
    row = lax.broadcasted_iota(jnp.int32, gt.shape, 0)
    is_fr = ((row >= 8) & (row < 16)) | (row >= 24)
    gt = jnp.where(is_fr, _log_sigmoid(gt), gt)
    gt_ref[...] = gt
    tm = kg.shape[0]
    r = lax.broadcasted_iota(jnp.int32, (tm, tm), 0)
    c = lax.broadcasted_iota(jnp.int32, (tm, tm), 1)
    same = (r // A_CHUNK) == (c // A_CHUNK)
    lower = jnp.where(same & (c <= r), 1.0, 0.0).astype(BF16)
    upper = jnp.where(same & (c >= r), 1.0, 0.0).astype(BF16)
    g, gal = kg[:, 64:96], kg[:, 96:128]
    fwd_l = lax.broadcasted_iota(jnp.int32, g.shape, 1) < 2 * A_HEADS
    b_al = jnp.where(fwd_l, _dot3r(lower, gal), _dot3r(upper, gal))
    y = jnp.concatenate([g - b_al, jnp.zeros((tm, LANE - 4 * A_HEADS), F32)], axis=1)
    cm = jnp.where(fwd_l, _cummax_rows(y, False, A_CHUNK)[:, :4 * A_HEADS],
                   _cummax_rows(y, True, A_CHUNK)[:, :4 * A_HEADS])
    kb_ref[...] = jnp.concatenate([b_al, cm, jnp.zeros((tm, LANE - 8 * A_HEADS), F32)], axis=1)
    fwd_r = lax.broadcasted_iota(jnp.int32, gt.shape, 0) < 2 * A_HEADS
    bt_ref[...] = jnp.where(fwd_r, _dot3(gt, upper), _dot3(gt, lower))


def _inproj(x2, g, wa, wc, wkg, wg, wkt, wgt, bkg, bgt, tm):
    T = x2.shape[0]
    row = lambda w: pl.BlockSpec((tm, w), lambda i: (i, 0))
    col = lambda r: pl.BlockSpec((r, tm), lambda i: (0, i))
    return pl.pallas_call(
        _inproj_kernel,
        grid=(T // tm,),
        in_specs=[row(D_MODEL), _const_spec(g.shape), _const_spec(wa.shape), _const_spec(wc.shape),
                  _const_spec(wkg.shape), _const_spec(wg.shape), _const_spec(wkt.shape),
                  _const_spec(wgt.shape), _const_spec(bkg.shape), _const_spec(bgt.shape)],
        out_specs=[row(1024), row(1024), row(1024), row(640), row(128), row(2048), col(1024), col(32), row(128),
                   col(32)],
        out_shape=[jax.ShapeDtypeStruct((T, 1024), BF16), jax.ShapeDtypeStruct((T, 1024), BF16),
                   jax.ShapeDtypeStruct((T, 1024), BF16), jax.ShapeDtypeStruct((T, 640), F32),
                   jax.ShapeDtypeStruct((T, 128), F32), jax.ShapeDtypeStruct((T, 2048), BF16),
                   jax.ShapeDtypeStruct((1024, T), BF16), jax.ShapeDtypeStruct((32, T), F32),
                   jax.ShapeDtypeStruct((T, 128), F32), jax.ShapeDtypeStruct((32, T), F32)],
        compiler_params=_cparams("parallel"),
        name="inproj",
    )(x2, g, wa, wc, wkg, wg, wkt, wgt, bkg, bgt)


def _mlstm_kernel(*refs, reverse):
    if reverse:
        q_ref, kt_ref, v_ref, kb_ref, gt_ref, bt_ref, hf_ref, os_ref, ng_ref, out_ref, cn_scr, m_scr = refs
    else:
        q_ref, kt_ref, v_ref, kb_ref, gt_ref, bt_ref, out_ref, cn_scr, m_scr = refs
    L = A_CHUNK

    @pl.when(pl.program_id(1) == 0)
    def _():
        cn_scr[...] = jnp.zeros_like(cn_scr)
        m_scr[...] = jnp.zeros_like(m_scr)

    row = lax.broadcasted_iota(jnp.int32, (L, L), 0)
    col = lax.broadcasted_iota(jnp.int32, (L, L), 1)
    mask = (col >= row) if reverse else (col <= row)
    gt = gt_ref[...]
    b_rows = bt_ref[...]
    bc3 = jnp.concatenate(_split3(kb_ref[:, :8 * A_HEADS]), axis=1)
    off = 16 if reverse else 0
    ones = jnp.ones((L, LANE), BF16)
    sel_r = lax.broadcasted_iota(jnp.int32, (6 * 4 * A_HEADS, 2 * LANE), 0)
    sel_c = lax.broadcasted_iota(jnp.int32, (6 * 4 * A_HEADS, 2 * LANE), 1)
    sel_r = sel_r % (8 * A_HEADS) - jnp.where(sel_c >= LANE, 4 * A_HEADS, 0)
    last = 0 if reverse else L - 1

    H = range(A_HEADS)
    sls = [slice(h * LANE, (h + 1) * LANE) for h in H]
    bc = [_dot(bc3, jnp.where(sel_r == off + h, 1.0, 0.0).astype(BF16)) for h in H]
    b_c = [x[:, :LANE] for x in bc]
    cm_c = [x[:, LANE:] for x in bc]
    qk = [_dot(q_ref[:, sls[h]], kt_ref[sls[h], :]) for h in H]
    cns = [cn_scr[h] for h in H]
    qc = [_dot(q_ref[:, sls[h]], cns[h].astype(BF16)) for h in H]
    ms = [m_scr[h] for h in H]
    i_r = [gt[off + h:off + h + 1, :] for h in H]
    b_r = [b_rows[off + 8 + h:off + 9 + h, :] for h in H]
    m_t = [b_c[h] + jnp.maximum(cm_c[h], ms[h]) for h in H]
    s = [(qk[h] * jnp.where(mask, jnp.exp(b_c[h] - b_r[h] + i_r[h] - m_t[h]), 0.0)).astype(BF16) for h in H]
    vext = [jnp.concatenate([v_ref[:, sls[h]], ones], axis=1) for h in H]
    sv = [_dot(s[h], vext[h]) for h in H]
    b_last = [b_c[h][last:last + 1] for h in H]
    m_new = [b_last[h] + jnp.maximum(ms[h], cm_c[h][last:last + 1]) for h in H]
    kts = [(kt_ref[sls[h], :].astype(F32) * jnp.exp(b_last[h] - b_r[h] + i_r[h] - m_new[h])).astype(BF16) for h in H]
    kv = [_dot(kts[h], vext[h]) for h in H]
    for h in H:
        decay = jnp.exp(b_last[h] + ms[h] - m_new[h])
        cn_scr[h] = jnp.concatenate([decay, decay], axis=1) * cns[h] + kv[h]
        m_scr[h] = m_new[h]
        inter_w = jnp.exp(b_c[h] + ms[h] - m_t[h])
        tot = sv[h] + jnp.concatenate([inter_w, inter_w], axis=1) * qc[h]
        hh = tot[:, :LANE] / jnp.maximum(jnp.abs(tot[:, LANE:]), jnp.exp(-m_t[h]))
        if reverse:
            t = hh + hf_ref[:, sls[h]]
            y = t * lax.rsqrt(jnp.mean(t * t, axis=-1, keepdims=True) + EPS) * ng_ref[:, sls[h]]
            out_ref[:, sls[h]] = (y * os_ref[:, sls[h]].astype(F32)).astype(BF16)
        else:
            out_ref[:, sls[h]] = hh


def _mlstm(qa, kt, va, kb, gt, bt, B, S, reverse, hf=None, osig=None, ng=None):
    L = A_CHUNK
    nc = S // L
    T = B * S
    if reverse:
        blk = lambda b, c: b * nc + (nc - 1 - c)
    else:
        blk = lambda b, c: b * nc + c
    row = lambda w: pl.BlockSpec((L, w), lambda b, c: (blk(b, c), 0))
    col = lambda r: pl.BlockSpec((r, L), lambda b, c: (0, blk(b, c)))
    in_specs = [row(1024), col(1024), row(1024), row(128), col(32), col(32)]
    args = [qa, kt, va, kb, gt, bt]
    if reverse:
        in_specs += [row(1024), row(1024), _const_spec(ng.shape)]
        args += [hf, osig, ng]
    return pl.pallas_call(
        functools.partial(_mlstm_kernel, reverse=reverse),
        grid=(B, nc),
        in_specs=in_specs,
        out_specs=row(1024),
        out_shape=jax.ShapeDtypeStruct((T, 1024), BF16 if reverse else F32),
        scratch_shapes=[pltpu.VMEM((A_HEADS, LANE, 2 * LANE), F32), pltpu.VMEM((A_HEADS, 1, LANE), F32)],
        compiler_params=_cparams("parallel", "arbitrary"),
        name="mlstm_bwd" if reverse else "mlstm_fwd",
    )(*args)


def _rope(x, cos, sin_signed, axis):
    idx = lax.broadcasted_iota(jnp.int32, x.shape, axis)
    half = B_ROPE // 2
    rot = jnp.where(idx < half, pltpu.roll(x, LANE - half, axis), pltpu.roll(x, half, axis))
    return x * cos + rot * sin_signed


BOUND_COL = B_DQK - LANE
LOG2E = 1.4426950408889634


def _mlaprep_kernel(cqkv_ref, kg_ref, cos_ref, sin_ref, cost_ref, sint_ref, gq_ref, gkv_ref, wq_ref, wkt_ref,
                    wv_ref, nq_ref, nkt_ref, sb_ref, q_ref, kt_ref, v_ref):
    ts = cqkv_ref.shape[0]
    cq = _rms(cqkv_ref[:, :Q_LORA], gq_ref[...]).astype(BF16)
    ckv = _rms(cqkv_ref[:, Q_LORA:], gkv_ref[...]).astype(BF16)
    kgt = kg_ref[...].T
    rowi = lax.broadcasted_iota(jnp.int32, kgt.shape, 0)
    krt = jnp.where(rowi < B_ROPE, kgt, 0.0)
    ss_r = jnp.sum(krt * krt, axis=0, keepdims=True)
    nkt = nkt_ref[...]
    krt_rot = _rope(krt * nkt[LANE:], cost_ref[...], sint_ref[...], 0)
    one_row = jnp.where(rowi == BOUND_COL, 1.0, 0.0)
    nq = nq_ref[...]
    cos, sin = cos_ref[...], sin_ref[...]
    lane = lax.broadcasted_iota(jnp.int32, (ts, LANE), 1)
    scale = B_DQK ** -0.5 * LOG2E
    neg_bound = jnp.broadcast_to(-sb_ref[...], (ts, LANE))
    ones = jnp.ones((ts, B_DV), BF16)
    H = range(B_HEADS)
    qs = [_dot(cq, wq_ref[h]) for h in H]
    kns = [_dot_nt(wkt_ref[h], ckv) for h in H]
    rqs = [lax.rsqrt(jnp.sum(q * q, axis=-1, keepdims=True) * (1.0 / B_DQK) + EPS) * scale for q in qs]
    rks = [lax.rsqrt((jnp.sum(k * k, axis=0, keepdims=True) + ss_r) * (1.0 / B_DQK) + EPS) for k in kns]
    qns = [q * nq for q in qs]
    rots = [_rope(qn[:, LANE:], cos, sin, 1) for qn in qns]
    for h in H:
        q_ref[0, h, :, :LANE] = (qns[h][:, :LANE] * rqs[h]).astype(BF16)
        q_ref[0, h, :, LANE:] = jnp.where(lane == BOUND_COL, neg_bound, rots[h] * rqs[h]).astype(BF16)
        kt_ref[0, h, :LANE, :] = (kns[h] * nkt[:LANE] * rks[h]).astype(BF16)
        kt_ref[0, h, LANE:, :] = (krt_rot * rks[h] + one_row).astype(BF16)
        v_ref[0, h, :, :B_DV] = _dot(ckv, wv_ref[h]).astype(BF16)
        v_ref[0, h, :, B_DV:] = ones


def _mlaprep(cqkv, kg, cos, sin, cost, sint, gq, gkv, wq, wkt, wv, nq, nkt, sb, B, S, ts):
    ns = S // ts
    row = lambda w: pl.BlockSpec((ts, w), lambda b, i: (b * ns + i, 0))
    pos = pl.BlockSpec((ts, LANE), lambda b, i: (i, 0))
    post = pl.BlockSpec((LANE, ts), lambda b, i: (0, i))
    consts = (gq, gkv, wq, wkt, wv, nq, nkt, sb)
    return pl.pallas_call(
        _mlaprep_kernel,
        grid=(B, ns),
        in_specs=[row(640), row(128), pos, pos, post, post] + [_const_spec(a.shape) for a in consts],
        out_specs=[pl.BlockSpec((1, B_HEADS, ts, QK_PAD), lambda b, i: (b, 0, i, 0)),
                   pl.BlockSpec((1, B_HEADS, QK_PAD, ts), lambda b, i: (b, 0, 0, i)),
                   pl.BlockSpec((1, B_HEADS, ts, 2 * B_DV), lambda b, i: (b, 0, i, 0))],
        out_shape=[jax.ShapeDtypeStruct((B, B_HEADS, S, QK_PAD), BF16),
                   jax.ShapeDtypeStruct((B, B_HEADS, QK_PAD, S), BF16),
                   jax.ShapeDtypeStruct((B, B_HEADS, S, 2 * B_DV), BF16)],
        compiler_params=_cparams("parallel", "parallel"),
        name="mlaprep",
    )(cqkv, kg, cos, sin, cost, sint, *consts)


def _attn_kernel(q_ref, kt_ref, v_ref, o_ref, *, tk, nk):
    q = q_ref[0, 0]
    tq = q.shape[0]

    def body(j, carry):
        m, acc = carry
        start = pl.multiple_of(j * tk, tk)
        kt = kt_ref[0, 0, :, pl.ds(start, tk)]
        v = v_ref[0, 0, pl.ds(start, tk), :]
        s = _dot(q, kt)
        m_new = jnp.maximum(m, jnp.max(s, axis=1, keepdims=True))
        p = jnp.exp2(s - m_new)
        acc = jnp.exp2(m - m_new) * acc + _dot(p.astype(BF16), v)
        return m_new, acc

    init = (jnp.full((tq, 1), -jnp.inf, F32), jnp.zeros((tq, 2 * B_DV), F32))
    _, acc = lax.fori_loop(0, nk, body, init)
    o_ref[0] = (acc[:, :B_DV] / acc[:, B_DV:]).astype(BF16)


def _attn_fast_kernel(q_ref, kt_ref, v_ref, o_ref, *, tk, nk, unroll):
    q = q_ref[0, 0]
    tq = q.shape[0]

    def probs(j):
        kt = kt_ref[0, 0, :, pl.ds(pl.multiple_of(j * tk, tk), tk)]
        return jnp.exp2(_dot(q, kt)).astype(BF16)

    def weighted(j, p, acc):
        v = v_ref[0, 0, pl.ds(pl.multiple_of(j * tk, tk), tk), :]
        return acc + _dot(p, v)

    def body(j, carry):
        p, acc = carry
        return probs(j), weighted(j - 1, p, acc)

    p, acc = lax.fori_loop(1, nk, body, (probs(0), jnp.zeros((tq, 2 * B_DV), F32)), unroll=unroll)
    acc = weighted(nk - 1, p, acc)
    o_ref[0] = (acc[:, :B_DV] / acc[:, B_DV:]).astype(BF16)


def _attention(q, kt, v, tq, tk, fast):
    B, H, S, _ = q.shape
    nk = S // tk
    if fast:
        body, name = functools.partial(_attn_fast_kernel, tk=tk, nk=nk, unroll=max(1, min(8, nk - 1))), "attention_fast"
    else:
        body, name = functools.partial(_attn_kernel, tk=tk, nk=nk), "attention"
    return pl.pallas_call(
        body,
        grid=(B, H, S // tq),
        in_specs=[pl.BlockSpec((1, 1, tq, QK_PAD), lambda b, h, i: (b, h, i, 0)),
                  pl.BlockSpec((1, 1, QK_PAD, S), lambda b, h, i: (b, h, 0, 0)),
                  pl.BlockSpec((1, 1, S, 2 * B_DV), lambda b, h, i: (b, h, 0, 0))],
        out_specs=pl.BlockSpec((1, tq, B_DV), lambda b, h, i: (b, i, h)),
        out_shape=jax.ShapeDtypeStruct((B, S, H * B_DV), BF16),
        compiler_params=_cparams("parallel", "parallel", "arbitrary"),
        name=name,
    )(q, kt, v)


def _merge_kernel(ha_ref, ao_ref, gs_ref, x_ref, wpa_ref, wpb_ref, wo_ref, o_ref):
    ya = _dot(ha_ref[...], wpa_ref[...])
    yb = _dot(ao_ref[...], wpb_ref[...])
    merged = gs_ref[:, :D_MODEL].astype(F32) * ya + gs_ref[:, D_MODEL:].astype(F32) * yb
    o_ref[...] = x_ref[...] + _dot(merged.astype(BF16), wo_ref[...])


def _merge(ha, ao, gs, x2, wpa, wpb, wo, tm):
    T = x2.shape[0]
    row = lambda w: pl.BlockSpec((tm, w), lambda i: (i, 0))
    return pl.pallas_call(
        _merge_kernel,
        grid=(T // tm,),
        in_specs=[row(1024), row(1024), row(2048), row(1024)] + [_const_spec(w.shape) for w in (wpa, wpb, wo)],
        out_specs=row(1024),
        out_shape=jax.ShapeDtypeStruct((T, D_MODEL), F32),
        compiler_params=_cparams("parallel"),
        name="merge",
    )(ha, ao, gs, x2, wpa, wpb, wo)


N_EXTRACT = PEER_TOPK + 1
ROUTE_HEADS_PER_ITER = 8


def _top_rows(s, n):
    rows = []
    for _ in range(n):
        m = jnp.max(s, axis=0, keepdims=True)
        rows.append(m)
        s = jnp.where(s == m, -jnp.inf, s)
    return rows


def _stack8(rows, t):
    ri = lax.broadcasted_iota(jnp.int32, (8, t), 0)
    out = jnp.full((8, t), -jnp.inf, F32)
    for k, r in enumerate(rows):
        out = jnp.where(ri == k, r, out)
    return out


def _route_head(h, q_scr, k1, k2, p1_ref, p2_ref, th_ref):
    qh = q_scr[h]
    t = qh.shape[0]
    s1 = _dot_nt(k1, qh[:, :PEER_HALF])
    s2 = _dot_nt(k2, qh[:, PEER_HALF:])
    v1 = _top_rows(s1, N_EXTRACT)
    v2 = _top_rows(s2, N_EXTRACT)
    v2a, v2b, v2c = _stack8(v2[0:8], t), _stack8(v2[8:16], t), _stack8(v2[16:17], t)
    v1b, v1c = _stack8(v1[8:16], t), _stack8(v1[16:17], t)
    cand = jnp.concatenate([v1[i] + v2a for i in range(8)] +
                           [v1[0] + v2b, v1[0] + v2c, v1b + v2[0], v1c + v2[0]], axis=0)
    best = _top_rows(cand, N_EXTRACT)
    z = jnp.zeros_like(best[0])
    for b in best[:PEER_TOPK]:
        z = z + jnp.exp(b - best[0])
    theta = 0.5 * (best[PEER_TOPK - 1] + best[PEER_TOPK])
    m1, m2 = v1[0], v2[0]
    c = RSQRT2 / z
    p1_ref[h] = jnp.exp(s1 - m1) * c
    p2_ref[h] = jnp.exp(s2 - m2)
    th_ref[pl.ds(h, 1), :] = jnp.exp(theta - m1 - m2) * c


def _route_kernel(x_ref, g_ref, wq_ref, k1_ref, k2_ref, xt_ref, p1_ref, p2_ref, th_ref, q_scr):
    xn = _rms(x_ref[...], g_ref[...])
    xt_ref[...] = xn.T.astype(BF16)
    q = _dot(xn.astype(BF16), wq_ref[...])
    for h in range(PEER_HEADS):
        q_scr[h] = q[:, h * PEER_DQ:(h + 1) * PEER_DQ].astype(BF16)
    k1, k2 = k1_ref[...], k2_ref[...]

    def head_group(i, carry):
        for k in range(ROUTE_HEADS_PER_ITER):
            _route_head(ROUTE_HEADS_PER_ITER * i + k, q_scr, k1, k2, p1_ref, p2_ref, th_ref)
        return carry

    lax.fori_loop(0, PEER_HEADS // ROUTE_HEADS_PER_ITER, head_group, 0)


def _route(x1, g, wq, k1, k2, tr):
    T = x1.shape[0]
    return pl.pallas_call(
        _route_kernel,
        grid=(T // tr,),
        in_specs=[pl.BlockSpec((tr, D_MODEL), lambda i: (i, 0)), _const_spec(g.shape), _const_spec(wq.shape),
                  _const_spec(k1.shape), _const_spec(k2.shape)],
        out_specs=[pl.BlockSpec((D_MODEL, tr), lambda i: (0, i)),
                   pl.BlockSpec((PEER_HEADS, PEER_NKEYS, tr), lambda i: (0, 0, i)),
                   pl.BlockSpec((PEER_HEADS, PEER_NKEYS, tr), lambda i: (0, 0, i)),
                   pl.BlockSpec((PEER_HEADS, tr), lambda i: (0, i))],
        out_shape=[jax.ShapeDtypeStruct((D_MODEL, T), BF16),
                   jax.ShapeDtypeStruct((PEER_HEADS, PEER_NKEYS, T), F32),
                   jax.ShapeDtypeStruct((PEER_HEADS, PEER_NKEYS, T), F32),
                   jax.ShapeDtypeStruct((PEER_HEADS, T), F32)],
        scratch_shapes=[pltpu.VMEM((PEER_HEADS, tr, PEER_DQ), BF16)],
        compiler_params=_cparams("parallel"),
        name="route",
    )(x1, g, wq, k1, k2)


E1_PER_STEP = 32
RSQRT2 = 0.7071067811865476


def _peer_kernel(xt_ref, u_ref, vt_ref, p1_ref, p2_ref, th_ref, x1_ref, o_ref, acc_scr, g_scr):
    s = pl.program_id(0)
    nblk = PEER_EXPERTS // (E1_PER_STEP * PEER_NKEYS)

    @pl.when(s == 0)
    def _():
        acc_scr[...] = jnp.zeros_like(acc_scr)
        g_scr[...] = jnp.zeros_like(g_scr)

    @pl.when((lax.rem(s, nblk) == 1) & (s > 1))
    def _():
        o_ref[...] = x1_ref[...] + acc_scr[...].T
        acc_scr[...] = jnp.zeros_like(acc_scr)

    slot = lax.rem(s, 2)
    acc_scr[...] += _dot(vt_ref[...], g_scr[1 - slot])
    xt = xt_ref[...]
    for e in range(E1_PER_STEP):
        sl = slice(e * PEER_NKEYS, (e + 1) * PEER_NKEYS)
        a = _dot(u_ref[sl, :], xt)
        act = a * (1.0 + lax.erf(a))
        w = jnp.zeros_like(a)
        for h in range(PEER_HEADS):
            p = p2_ref[h] * p1_ref[h, e:e + 1, :]
            w = w + jnp.where(p >= th_ref[h:h + 1, :], p, 0.0)
        g_scr[slot, sl, :] = (w * act).astype(BF16)


def _peer(xt, u, vt, p1, p2, th, x1, tb):
    T = x1.shape[0]
    eb = E1_PER_STEP * PEER_NKEYS
    nblk = PEER_EXPERTS // eb
    nt = T // tb
    cur = lambda s: jnp.minimum(s // nblk, nt - 1)
    done = lambda s: jnp.maximum(s - 2, 0) // nblk
    return pl.pallas_call(
        _peer_kernel,
        grid=(nt * nblk + 2,),
        in_specs=[pl.BlockSpec((D_MODEL, tb), lambda s: (0, cur(s))),
                  pl.BlockSpec((eb, D_MODEL), lambda s: (s % nblk, 0)),
                  pl.BlockSpec((D_MODEL, eb), lambda s: (0, (s + nblk - 1) % nblk)),
                  pl.BlockSpec((PEER_HEADS, E1_PER_STEP, tb), lambda s: (0, s % nblk, cur(s))),
                  pl.BlockSpec((PEER_HEADS, PEER_NKEYS, tb), lambda s: (0, 0, cur(s))),
                  pl.BlockSpec((PEER_HEADS, tb), lambda s: (0, cur(s))),
                  pl.BlockSpec((tb, D_MODEL), lambda s: (done(s), 0))],
        out_specs=pl.BlockSpec((tb, D_MODEL), lambda s: (done(s), 0)),
        out_shape=jax.ShapeDtypeStruct((T, D_MODEL), F32),
        scratch_shapes=[pltpu.VMEM((D_MODEL, tb), F32), pltpu.VMEM((2, eb, tb), BF16)],
        compiler_params=_cparams("arbitrary"),
        name="peer",
    )(xt, u, vt, p1, p2, th, x1)


def _prep_weights(p):
    w_in = p["w_in"]
    sizes = (A_HEADS * A_DQK, A_HEADS * A_DQK, A_HEADS * A_DV, A_HEADS * A_DV, 4 * A_HEADS,
             Q_LORA, KV_LORA, B_ROPE, 2 * D_MODEL)
    pts = np.cumsum((0,) + sizes)
    w_q, w_k, w_v, w_o, w_gate, w_cq, w_ckv, w_kr, w_gbr = (w_in[:, pts[i]:pts[i + 1]] for i in range(9))

    def pad_heads(w):
        w = w.reshape(D_MODEL, A_HEADS, A_DQK)
        return jnp.pad(w, ((0, 0), (0, 0), (0, LANE - A_DQK))).reshape(D_MODEL, A_HEADS * LANE)

    out = {}
    out["wa"] = jnp.concatenate([pad_heads(w_q) * (A_DQK ** -0.5), w_v, w_o], axis=1).astype(BF16)
    out["wc"] = jnp.concatenate([w_cq, w_ckv], axis=1).astype(BF16)
    wg4 = w_gate.reshape(D_MODEL, 4, A_HEADS)
    w_fal = jnp.stack([wg4[:, 1], wg4[:, 1], wg4[:, 3], wg4[:, 3]], axis=1).reshape(D_MODEL, 4 * A_HEADS)
    out["wkg"] = jnp.concatenate([w_kr, w_gate, w_fal], axis=1).astype(BF16)
    out["wg"] = w_gbr.astype(BF16)
    out["wkt"] = pad_heads(w_k).T.astype(BF16)
    out["wgt"] = w_gate.T.astype(BF16)
    bias = p["b_mgate"].astype(F32).reshape(4 * A_HEADS)
    b4 = bias.reshape(4, A_HEADS)
    b_fal = jnp.stack([b4[1], b4[1], b4[3], b4[3]]).reshape(4 * A_HEADS)
    out["bkg"] = jnp.concatenate([jnp.zeros((B_ROPE,), F32), bias, b_fal]).reshape(1, LANE)
    out["bgt"] = bias.reshape(4 * A_HEADS, 1)
    out["g_mix"] = p["norm_mix_g"].astype(F32).reshape(1, D_MODEL)
    out["g_mlstm"] = p["mlstm_norm_g"].astype(F32).reshape(1, A_HEADS * A_DV)
    out["gq"] = p["q_a_norm_g"].astype(F32).reshape(1, Q_LORA)
    out["gkv"] = p["kv_a_norm_g"].astype(F32).reshape(1, KV_LORA)
    wq = p["w_q_up"].reshape(Q_LORA, B_HEADS, B_DQK).transpose(1, 0, 2)
    out["wq"] = jnp.pad(wq, ((0, 0), (0, 0), (0, QK_PAD - B_DQK))).astype(BF16)
    wkv = p["w_kv_up"].reshape(KV_LORA, B_HEADS, B_NOPE + B_DV).transpose(1, 0, 2)
    out["wkbt"] = wkv[:, :, :B_NOPE].transpose(0, 2, 1).astype(BF16)
    out["wv"] = wkv[:, :, B_NOPE:].astype(BF16)
    out["nq"] = jnp.pad(p["qk_norm_q_g"].astype(F32), (0, QK_PAD - B_DQK)).reshape(1, QK_PAD)
    nk = p["qk_norm_k_g"].astype(F32)
    out["nk_col"] = jnp.pad(nk, (0, QK_PAD - B_DQK)).reshape(QK_PAD, 1)
    out["score_bound"] = (1.01 * B_DQK ** 0.5 * LOG2E * jnp.max(jnp.abs(nk))
                          * jnp.max(jnp.abs(p["qk_norm_q_g"]))).astype(F32).reshape(1, 1)
    out["score_bound_row"] = jnp.broadcast_to(out["score_bound"], (1, LANE))
    out["wpa"] = p["w_proj_a"].astype(BF16)
    out["wpb"] = p["w_proj_b"].astype(BF16)
    out["wo"] = p["w_out"].astype(BF16)
    out["g_ffn"] = p["norm_ffn_g"].astype(F32).reshape(1, D_MODEL)
    out["wpq"] = p["w_peer_q"].astype(BF16)
    out["k1"] = p["peer_keys1"].astype(BF16)
    out["k2"] = p["peer_keys2"].astype(BF16)
    out["u"] = (p["peer_u"] * RSQRT2).astype(BF16)
    out["vt"] = p["peer_v"].T.astype(BF16)
    return out


def _rope_tables(S):
    pos = jnp.arange(S, dtype=F32)
    inv = ROPE_THETA ** (-jnp.arange(0, B_ROPE, 2, dtype=F32) / B_ROPE)
    ang = pos[:, None] * inv[None, :]
    ang = jnp.concatenate([ang, ang], axis=-1)
    sign = jnp.where(jnp.arange(B_ROPE) < B_ROPE // 2, -1.0, 1.0).astype(F32)
    pad = ((0, 0), (0, LANE - B_ROPE))
    cos, sin = jnp.pad(jnp.cos(ang), pad), jnp.pad(jnp.sin(ang) * sign, pad)
    return cos, sin, cos.T, sin.T


def _pick(n, pref):
    t = min(n, pref)
    assert n % t == 0, (n, t)
    return t


def _layer(x, w):
    B, S, _ = x.shape
    T = B * S
    x2 = x.reshape(T, D_MODEL)
    qa, va, osig, cqkv, kg, gs, kt, gt, kb, bt = _inproj(
        x2, w["g_mix"], w["wa"], w["wc"], w["wkg"], w["wg"], w["wkt"], w["wgt"], w["bkg"], w["bgt"],
        _pick(T, 256))
    hf = _mlstm(qa, kt, va, kb, gt, bt, B, S, False)
    ha = _mlstm(qa, kt, va, kb, gt, bt, B, S, True, hf, osig, w["g_mlstm"])
    cos, sin, cost, sint = _rope_tables(S)
    ts = _pick(S, 256)
    nkt = jnp.broadcast_to(w["nk_col"], (QK_PAD, ts))
    q, kbt, v = _mlaprep(cqkv, kg, cos, sin, cost, sint, w["gq"], w["gkv"], w["wq"], w["wkbt"], w["wv"], w["nq"],
                         nkt, w["score_bound_row"], B, S, ts)
    tq, tk = _pick(S, 2048), _pick(S, 512)
    ao = lax.cond(w["score_bound"][0, 0] <= 50.0,
                  lambda *a: _attention(*a, tq, tk, True),
                  lambda *a: _attention(*a, tq, tk, False),
                  q, kbt, v).reshape(T, B_HEADS * B_DV)
    x1 = _merge(ha, ao, gs, x2, w["wpa"], w["wpb"], w["wo"], _pick(T, 512))
    xt, p1, p2, th = _route(x1, w["g_ffn"], w["wpq"], w["k1"], w["k2"], _pick(T, 256))
    y = _peer(xt, w["u"], w["vt"], p1, p2, th, x1, _pick(T, 256))
    return y.reshape(B, S, D_MODEL)


def kernel(x_prompt, x_sample, norm_mix_g, w_in, b_mgate, mlstm_norm_g, q_a_norm_g, w_q_up, kv_a_norm_g, w_kv_up, qk_norm_q_g, qk_norm_k_g, w_proj_a, w_proj_b, w_out, norm_ffn_g, w_peer_q, peer_keys1, peer_keys2, peer_u, peer_v):
    params = dict(norm_mix_g=norm_mix_g, w_in=w_in, b_mgate=b_mgate, mlstm_norm_g=mlstm_norm_g,
                  q_a_norm_g=q_a_norm_g, w_q_up=w_q_up, kv_a_norm_g=kv_a_norm_g, w_kv_up=w_kv_up,
                  qk_norm_q_g=qk_norm_q_g, qk_norm_k_g=qk_norm_k_g, w_proj_a=w_proj_a, w_proj_b=w_proj_b,
                  w_out=w_out, norm_ffn_g=norm_ffn_g, w_peer_q=w_peer_q, peer_keys1=peer_keys1,
                  peer_keys2=peer_keys2, peer_u=peer_u, peer_v=peer_v)
    depth = w_in.shape[0]
    layers = [_prep_weights({k: v[l] for k, v in params.items()}) for l in range(depth)]

    def run(x):
        for w in layers:
            x = _layer(x, w)
        return x

    return (run(x_prompt), run(x_sample))
```
